```python
import math
import jax, jax.numpy as jnp
from jax import lax
import numpy as np

D_MODEL = 1024
BATCH = 8
SEQ = 4096
DEPTH = 1
DEC_BATCH = 8
DEC_SEQ = 16
PAST_LEN = 4096

CHUNK = 64
EPS = 1e-6
GLA_HEADS = 4
GLA_QK = D_MODEL // 2
GLA_V = D_MODEL
GLA_DK = GLA_QK // GLA_HEADS
GLA_DV = GLA_V // GLA_HEADS
GLA_RANK = 16
GLA_GATE_NORM = 16.0
GDN_HEADS = 8
GDN_DK = 128
GDN_DV = 128
GDN_QK = GDN_HEADS * GDN_DK
GDN_V = GDN_HEADS * GDN_DV
GDN_CONV_CH = 2 * GDN_QK + GDN_V
CONV_W = 4
SPLIT_SIZES = (GLA_QK, GLA_QK, GLA_V, GLA_V, GLA_RANK, GDN_CONV_CH, GDN_V, GDN_HEADS, GDN_HEADS, D_MODEL, D_MODEL)
D_IN = 2 * GLA_QK + 2 * GLA_V + GLA_RANK + GDN_CONV_CH + GDN_V + 2 * GDN_HEADS + 2 * D_MODEL

kernel_name = "gla_gdn_parallel_streaming_step"


def rmsnorm(x, gain):
    xf = x.astype(jnp.float32)
    y = xf * lax.rsqrt(jnp.mean(xf * xf, axis=-1, keepdims=True) + EPS)
    return (y * gain.astype(jnp.float32)).astype(x.dtype)


def l2norm(x):
    return x * lax.rsqrt(jnp.sum(x * x, axis=-1, keepdims=True) + EPS)


def pad_time(a, pad):
    return jnp.pad(a, [(0, 0), (0, pad)] + [(0, 0)] * (a.ndim - 2))


def to_blocks(a):
    b, t = a.shape[0], a.shape[1]
    a = a.reshape((b, t // CHUNK, CHUNK) + a.shape[2:])
    return jnp.moveaxis(a, 2, 3)


def from_blocks(a):
    a = jnp.moveaxis(a, 3, 2)
    return a.reshape((a.shape[0], a.shape[1] * CHUNK) + a.shape[3:])


def gla_chunked(q, k, v, g, s0):
    qc, kc, vc, gc = (to_blocks(a) for a in (q, k, v, g))
    b = jnp.cumsum(gc, axis=3)
    b_ref = b[:, :, :, CHUNK // 2 - 1:CHUNK // 2, :]
    b_last = b[:, :, :, CHUNK - 1:, :]
    incl = jnp.tril(jnp.ones((CHUNK, CHUNK), dtype=bool))
    att = jnp.einsum('bnhid,bnhjd->bnhij', qc * jnp.exp(b - b_ref), kc * jnp.exp(b_ref - b))
    att = jnp.where(incl, att, 0.0)
    o_intra = jnp.einsum('bnhij,bnhjv->bnhiv', att, vc)
    q_dec = qc * jnp.exp(b)
    k_dec = kc * jnp.exp(b_last - b)
    block_decay = jnp.exp(b_last[:, :, :, 0, :])

    def step(s, inp):
        q_n, k_n, v_n, d_n = inp
        o_n = jnp.einsum('bhid,bhdv->bhiv', q_n, s)
        s = d_n[..., None] * s + jnp.einsum('bhjd,bhjv->bhdv', k_n, v_n)
        return s, o_n

    xs = tuple(jnp.moveaxis(a, 1, 0) for a in (q_dec, k_dec, vc, block_decay))
    s_fin, o_inter = lax.scan(step, s0, xs)
    o = o_intra + jnp.moveaxis(o_inter, 0, 1)
    return from_blocks(o), s_fin


def gdn_chunked(q, k, v, beta, g, s0):
    dv = v.shape[-1]
    qc, kc, vc, bc, gc = (to_blocks(a) for a in (q, k, v, beta, g))
    b = jnp.cumsum(gc, axis=-1)
    incl = jnp.tril(jnp.ones((CHUNK, CHUNK), dtype=bool))
    strict = jnp.tril(jnp.ones((CHUNK, CHUNK), dtype=bool), -1)
    diff = b[..., :, None] - b[..., None, :]
    decay_mat = jnp.where(incl, jnp.exp(jnp.where(incl, diff, 0.0)), 0.0)
    kk = jnp.einsum('bnhid,bnhjd->bnhij', kc, kc)
    a_low = jnp.where(strict, bc[..., :, None] * kk * decay_mat, 0.0)
    eye = jnp.eye(CHUNK, dtype=a_low.dtype)
    rhs = jnp.concatenate([bc[..., None] * vc, (bc * jnp.exp(b))[..., None] * kc], axis=-1)
    sol = lax.linalg.triangular_solve(a_low + eye, rhs, left_side=True, lower=True, unit_diagonal=True)
    u_v, w_k = sol[..., :dv], sol[..., dv:]
    qk = jnp.einsum('bnhid,bnhjd->bnhij', qc, kc) * decay_mat
    q_dec = qc * jnp.exp(b)[..., None]
    k_dec = kc * jnp.exp(b[..., -1:] - b)[..., None]
    block_decay = jnp.exp(b[..., -1])

    def step(s, inp):
        q_n, k_n, uv_n, wk_n, qk_n, d_n = inp
        u = uv_n - jnp.einsum('bhid,bhdv->bhiv', wk_n, s)
        o_n = jnp.einsum('bhid,bhdv->bhiv', q_n, s) + jnp.einsum('bhij,bhjv->bhiv', qk_n, u)
        s = d_n[..., None, None] * s + jnp.einsum('bhjd,bhjv->bhdv', k_n, u)
        return s, o_n

    xs = tuple(jnp.moveaxis(a, 1, 0) for a in (q_dec, k_dec, u_v, w_k, qk, block_decay))
    s_fin, o = lax.scan(step, s0, xs)
    return from_blocks(jnp.moveaxis(o, 0, 1)), s_fin


def causal_conv(u, buf, w):
    t = u.shape[1]
    up = jnp.concatenate([buf, u], axis=1)
    y = up[:, 0:t] * w[0]
    for i in range(1, CONV_W):
        y = y + up[:, i:i + t] * w[i]
    return jax.nn.silu(y), up[:, t:]


def head_gated_norm(o, gain, z):
    of = o.astype(jnp.float32)
    of = of * lax.rsqrt(jnp.mean(of * of, axis=-1, keepdims=True) + EPS) * gain.astype(jnp.float32)
    return of * jax.nn.silu(z)


def streaming_layer(x, c, s_gla, s_gdn, conv_buf, w_ada, b_ada, g_norm1, w_in, w_gk2, b_gk,
                    w_conv, a_log, dt_bias, g_norm_a, g_norm_b, w_pa, w_pb, w_out):
    bsz, t, _ = x.shape
    pad = (-t) % CHUNK
    f32 = jnp.float32
    mod = jax.nn.silu(c) @ w_ada + b_ada
    shift, scale, gate = jnp.split(mod, 3, axis=-1)
    h = rmsnorm(x, g_norm1) * (1.0 + scale[:, None, :]) + shift[:, None, :]
    p = (h @ w_in).astype(f32)
    split_idx = [int(i) for i in np.cumsum(SPLIT_SIZES)[:-1]]
    qa, ka, va, za, gk_low, qkv_b, zb, beta_in, a_in, ga, gb = jnp.split(p, split_idx, axis=-1)

    gk = jax.nn.log_sigmoid(gk_low @ w_gk2 + b_gk) / GLA_GATE_NORM
    qa = qa.reshape(bsz, t, GLA_HEADS, GLA_DK) * (GLA_DK ** -0.5)
    ka = ka.reshape(bsz, t, GLA_HEADS, GLA_DK)
    va = va.reshape(bsz, t, GLA_HEADS, GLA_DV)
    gk = gk.reshape(bsz, t, GLA_HEADS, GLA_DK)
    oa, s_gla_new = gla_chunked(pad_time(qa, pad), pad_time(ka, pad), pad_time(va, pad),
                                pad_time(gk, pad), s_gla.astype(f32))
    oa = head_gated_norm(oa[:, :t], g_norm_a, za.reshape(bsz, t, GLA_HEADS, GLA_DV))
    oa = oa.reshape(bsz, t, GLA_V)

    qkv_c, conv_new = causal_conv(qkv_b, conv_buf.astype(f32), w_conv)
    qb, kb, vb = jnp.split(qkv_c, [GDN_QK, 2 * GDN_QK], axis=-1)
    qb = l2norm(qb.reshape(bsz, t, GDN_HEADS, GDN_DK)) * (GDN_DK ** -0.5)
    kb = l2norm(kb.reshape(bsz, t, GDN_HEADS, GDN_DK))
    vb = vb.reshape(bsz, t, GDN_HEADS, GDN_DV)
    beta = jax.nn.sigmoid(beta_in)
    g_b = -jnp.exp(a_log.astype(f32)) * jax.nn.softplus(a_in + dt_bias)
    ob, s_gdn_new = gdn_chunked(pad_time(qb, pad), pad_time(kb, pad), pad_time(vb, pad),
                                pad_time(beta, pad), pad_time(g_b, pad), s_gdn.astype(f32))
    ob = head_gated_norm(ob[:, :t], g_norm_b, zb.reshape(bsz, t, GDN_HEADS, GDN_DV))
    ob = ob.reshape(bsz, t, GDN_V)

    merged = jax.nn.sigmoid(ga) * (oa @ w_pa) + jax.nn.sigmoid(gb) * (ob @ w_pb)
    out = (merged @ w_out).astype(x.dtype)
    x = x + gate[:, None, :] * out
    return x, s_gla_new.astype(s_gla.dtype), s_gdn_new.astype(s_gdn.dtype), conv_new.astype(conv_buf.dtype)


def setup_inputs(seed: int = 0) -> dict:
    key = jax.random.key(seed)
    ks = jax.random.split(key, 24)

    def nrm(k, shape, s):
        return jax.random.normal(k, shape, jnp.float32) * s

    dt = jnp.exp(jax.random.uniform(ks[15], (DEPTH, GDN_HEADS), jnp.float32, math.log(1e-3), math.log(1e-1)))
    return {
        "x_prompt": nrm(ks[0], (BATCH, SEQ, D_MODEL), 1.0),
        "x_sample": nrm(ks[1], (DEC_BATCH, DEC_SEQ, D_MODEL), 1.0),
        "c_prompt": nrm(ks[2], (BATCH, D_MODEL), 1.0),
        "c_sample": nrm(ks[3], (DEC_BATCH, D_MODEL), 1.0),
        "state_gla": nrm(ks[4], (DEPTH, DEC_BATCH, GLA_HEADS, GLA_DK, GLA_DV), 1.0),
        "state_gdn": nrm(ks[5], (DEPTH, DEC_BATCH, GDN_HEADS, GDN_DK, GDN_DV), 0.3),
        "cache_conv_gdn": nrm(ks[6], (DEPTH, DEC_BATCH, CONV_W - 1, GDN_CONV_CH), 1.0),
        "w_ada": nrm(ks[7], (DEPTH, D_MODEL, 3 * D_MODEL), 0.5 * D_MODEL ** -0.5),
        "b_ada": nrm(ks[8], (DEPTH, 3 * D_MODEL), 0.02),
        "g_norm1": 1.0 + nrm(ks[9], (DEPTH, D_MODEL), 0.02),
        "w_in": nrm(ks[10], (DEPTH, D_MODEL, D_IN), D_MODEL ** -0.5),
        "w_gk2": nrm(ks[11], (DEPTH, GLA_RANK, GLA_QK), GLA_RANK ** -0.5),
        "b_gk": nrm(ks[12], (DEPTH, GLA_QK), 0.1),
        "w_conv": nrm(ks[13], (DEPTH, CONV_W, GDN_CONV_CH), CONV_W ** -0.5),
        "a_log": jnp.log(jax.random.uniform(ks[14], (DEPTH, GDN_HEADS), jnp.float32, 1.0, 16.0)),
        "dt_bias": dt + jnp.log(-jnp.expm1(-dt)),
        "g_norm_a": 1.0 + nrm(ks[16], (DEPTH, GLA_DV), 0.02),
        "g_norm_b": 1.0 + nrm(ks[17], (DEPTH, GDN_DV), 0.02),
        "w_pa": nrm(ks[18], (DEPTH, GLA_V, D_MODEL), GLA_V ** -0.5),
        "w_pb": nrm(ks[19], (DEPTH, GDN_V, D_MODEL), GDN_V ** -0.5),
        "w_out": nrm(ks[20], (DEPTH, D_MODEL, D_MODEL), D_MODEL ** -0.5),
        "g_final": 1.0 + nrm(ks[21], (D_MODEL,), 0.02),
    }


def reference(x_prompt, x_sample, c_prompt, c_sample, state_gla, state_gdn, cache_conv_gdn,
              w_ada, b_ada, g_norm1, w_in, w_gk2, b_gk, w_conv, a_log, dt_bias,
              g_norm_a, g_norm_b, w_pa, w_pb, w_out, g_final):
    bp = x_prompt.shape[0]
    hp, hs = x_prompt, x_sample
    gla_p, gdn_p, conv_p, gla_s, gdn_s, conv_s = [], [], [], [], [], []
    for layer in range(DEPTH):
        lw = (w_ada[layer], b_ada[layer], g_norm1[layer], w_in[layer], w_gk2[layer], b_gk[layer],
              w_conv[layer], a_log[layer], dt_bias[layer], g_norm_a[layer], g_norm_b[layer],
              w_pa[layer], w_pb[layer], w_out[layer])
        z_gla = jnp.zeros((bp, GLA_HEADS, GLA_DK, GLA_DV), jnp.float32)
        z_gdn = jnp.zeros((bp, GDN_HEADS, GDN_DK, GDN_DV), jnp.float32)
        z_conv = jnp.zeros((bp, CONV_W - 1, GDN_CONV_CH), jnp.float32)
        hp, sg, sd, cv = streaming_layer(hp, c_prompt, z_gla, z_gdn, z_conv, *lw)
        gla_p.append(sg)
        gdn_p.append(sd)
        conv_p.append(cv)
        hs, sg, sd, cv = streaming_layer(hs, c_sample, state_gla[layer], state_gdn[layer], cache_conv_gdn[layer], *lw)
        gla_s.append(sg)
        gdn_s.append(sd)
        conv_s.append(cv)
    y_prompt = rmsnorm(hp, g_final)
    y_sample = rmsnorm(hs, g_final)
    return (y_prompt, y_sample, jnp.stack(gla_p), jnp.stack(gdn_p), jnp.stack(conv_p), jnp.stack(gla_s), jnp.stack(gdn_s), jnp.stack(conv_s))
```

```python
import functools

import jax
import jax.numpy as jnp
from jax import lax
from jax.experimental import pallas as pl
from jax.experimental.pallas import tpu as pltpu

F32 = jnp.float32
BF16 = jnp.bfloat16

CHUNK = 64
EPS = 1e-6
GLA_HEADS = 4
GLA_RANK = 16
GLA_GATE_NORM = 16.0
GDN_HEADS = 8
GDN_DK = 128
GDN_DV = 128
CONV_W = 4
LANES = 128
HIST = 8
INV_BASE = 8
BETA_LANE = GLA_RANK
AIN_LANE = GLA_RANK + GDN_HEADS
VMEM_LIMIT_BYTES = 56 * 1024 * 1024


def _dot(a, b):
    return jnp.dot(a, b, preferred_element_type=F32)


def _dot_nt(a, b):
    return lax.dot_general(a, b, (((1,), (1,)), ((), ())), preferred_element_type=F32)


def _dot_tn(a, b):
    return lax.dot_general(a, b, (((0,), (0,)), ((), ())), preferred_element_type=F32)


def _split3(x):
    hi = x.astype(BF16)
    r = x - hi.astype(F32)
    mid = r.astype(BF16)
    lo = (r - mid.astype(F32)).astype(BF16)
    return hi, mid, lo


def _softplus(x):
    return jnp.maximum(x, 0.0) + jnp.log1p(jnp.exp(-jnp.abs(x)))


def _log_sigmoid(x):
    return jnp.minimum(x, 0.0) - jnp.log1p(jnp.exp(-jnp.abs(x)))


def _silu(x):
    return x * jax.nn.sigmoid(x)


def _mod_kernel(c_ref, w_ref, b_ref, o_ref):
    s = _silu(c_ref[...]).astype(BF16)
    o_ref[...] = _dot(s, w_ref[...].astype(BF16)) + b_ref[...]


def _adaln_mod(c, w_ada, b_ada):
    n, d = c.shape
    d3 = w_ada.shape[1]
    bn = 512
    return pl.pallas_call(
        _mod_kernel,
        grid=(d3 // bn,),
        in_specs=[pl.BlockSpec((n, d), lambda j: (0, 0)),
                  pl.BlockSpec((d, bn), lambda j: (0, j)),
                  pl.BlockSpec((1, bn), lambda j: (0, j))],
        out_specs=pl.BlockSpec((n, bn), lambda j: (0, j)),
        out_shape=jax.ShapeDtypeStruct((n, d3), F32),
        name="adaln_mod",
    )(c, w_ada, b_ada.reshape(1, d3))


def _layer_kernel(*refs, tc, n_valid, has_state, final_norm, d_model):
    d = d_model
    qk_a = d // 2
    dk_a = qk_a // GLA_HEADS
    dv_a = d // GLA_HEADS
    nch = tc // CHUNK
    o_qa, o_ka, o_va, o_za = 0, qk_a, 2 * qk_a, 2 * qk_a + d
    o_qb = o_za + d
    o_kb, o_vb = o_qb + d, o_qb + 2 * d
    o_zb = o_qb + 3 * d
    o_ga, o_gb = o_zb + d, o_zb + 2 * d
    o_sm = o_gb + d

    it = iter(refs)
    x_ref, mod_ref, g1_ref, win_ref, wat_ref, wgk_ref, bgk_ref, wconv_ref = (next(it) for _ in range(8))
    alane_ref, dlane_ref, acol_ref, dcol_ref = (next(it) for _ in range(4))
    gna_ref, gnb_ref, wpa_ref, wpb_ref, wout_ref, gfin_ref, bd_ref, ut_ref = (next(it) for _ in range(8))
    if has_state:
        sgla_in, sgdn_in, conv_in = (next(it) for _ in range(3))
    y_ref, sgla_ref, sgdn_ref, conv_ref = (next(it) for _ in range(4))
    (hb_ref, qe_ref, ke_ref, qd_ref, kd_ref, va_ref, sza_ref, bdec_ref, ubuf_ref,
     qn_ref, kn_ref, qdec_ref, kdec_ref, bv_ref, bek_ref, szb_ref, sga_ref, sgb_ref,
     bcol_ref, betac_ref, brow_ref, dvec_ref, oa_ref, ob_ref) = (next(it) for _ in range(24))

    t = pl.program_id(1)

    @pl.when(t == 0)
    def _init():
        if has_state:
            sgla_ref[...] = sgla_in[...]
            sgdn_ref[...] = sgdn_in[...]
            ubuf_ref[HIST - (CONV_W - 1):HIST, :] = conv_in[0]
        else:
            sgla_ref[...] = jnp.zeros_like(sgla_ref)
            sgdn_ref[...] = jnp.zeros_like(sgdn_ref)
            ubuf_ref[0:HIST, :] = jnp.zeros((HIST, ubuf_ref.shape[1]), F32)

    masked = n_valid < tc
    if masked:
        rowmask = lax.broadcasted_iota(jnp.int32, (tc, 1), 0) < n_valid

    def mrow(v):
        return jnp.where(rowmask, v, 0.0) if masked else v

    x = x_ref[0]
    mod = mod_ref[0]
    shift, scale = mod[:, 0:d], mod[:, d:2 * d]
    hn = x * lax.rsqrt(jnp.mean(x * x, axis=-1, keepdims=True) + EPS) * g1_ref[...]
    hb_ref[...] = (hn * (1.0 + scale) + shift).astype(BF16)

    def proj(c0, width):
        return _dot(hb_ref[...], win_ref[:, c0:c0 + width])

    ps = proj(o_sm, LANES)
    lane = lax.broadcasted_iota(jnp.int32, (1, LANES), 1)
    betac_ref[...] = mrow(jax.nn.sigmoid(ps))
    g_col = -jnp.exp(alane_ref[...]) * _softplus(ps + dlane_ref[...])
    g_col = mrow(jnp.where((lane >= AIN_LANE) & (lane < AIN_LANE + GDN_HEADS), g_col, 0.0))
    bd = bd_ref[...]
    gh, gm, gl = _split3(g_col)
    b_col = _dot(bd, gh) + _dot(bd, gm) + _dot(bd, gl)
    bcol_ref[...] = b_col

    ut = ut_ref[...]
    for c in range(nch):
        arow = _dot_nt(wat_ref[...], hb_ref[c * CHUNK:(c + 1) * CHUNK, :])
        g_row = -jnp.exp(acol_ref[...]) * _softplus(arow + dcol_ref[...])
        if masked:
            colmask = (lax.broadcasted_iota(jnp.int32, (1, CHUNK), 1) + c * CHUNK) < n_valid
            g_row = jnp.where(colmask, g_row, 0.0)
        rh, rm, rl = _split3(g_row)
        b_row = _dot(rh, ut) + _dot(rm, ut) + _dot(rl, ut)
        brow_ref[c] = b_row
        dvec_ref[c] = jnp.broadcast_to(jnp.exp(b_row[:, CHUNK - 1:CHUNK]), (GDN_HEADS, LANES))

    gk = _log_sigmoid(_dot(ps.astype(BF16), wgk_ref[...]) + bgk_ref[...]) * (1.0 / GLA_GATE_NORM)
    gk = mrow(gk)
    kh, km, kl = _split3(gk)
    b_a = (_dot(bd, kh) + _dot(bd, km) + _dot(bd, kl)).reshape(nch, CHUNK, qk_a)
    b_mid = b_a[:, CHUNK // 2 - 1:CHUNK // 2, :]
    b_last = b_a[:, CHUNK - 1:CHUNK, :]
    bdec_ref[...] = jnp.exp(b_last)

    pqk = proj(o_qa, 2 * qk_a)
    qa = mrow(pqk[:, 0:qk_a] * (dk_a ** -0.5)).reshape(nch, CHUNK, qk_a)
    ka = mrow(pqk[:, qk_a:2 * qk_a]).reshape(nch, CHUNK, qk_a)
    qe_ref[...] = (qa * jnp.exp(b_a - b_mid)).reshape(tc, qk_a).astype(BF16)
    ke_ref[...] = (ka * jnp.exp(b_mid - b_a)).reshape(tc, qk_a).astype(BF16)
    qd_ref[...] = (qa * jnp.exp(b_a)).reshape(tc, qk_a).astype(BF16)
    kd_ref[...] = (ka * jnp.exp(b_last - b_a)).reshape(tc, qk_a).astype(BF16)

    va_ref[...] = mrow(proj(o_va, d)).astype(BF16)
    sza_ref[...] = _silu(proj(o_za, d))
    szb_ref[...] = _silu(proj(o_zb, d))
    sga_ref[...] = jax.nn.sigmoid(proj(o_ga, d))
    sgb_ref[...] = jax.nn.sigmoid(proj(o_gb, d))

    for j in range(3):
        ubuf_ref[HIST:HIST + tc, j * d:(j + 1) * d] = proj(o_qb + j * d, d)

    def conv(j):
        cols = slice(j * d, (j + 1) * d)
        acc = ubuf_ref[HIST - (CONV_W - 1):HIST - (CONV_W - 1) + tc, cols] * wconv_ref[0:1, cols]
        for i in range(1, CONV_W):
            s = HIST - (CONV_W - 1) + i
            acc = acc + ubuf_ref[s:s + tc, cols] * wconv_ref[i:i + 1, cols]
        return _silu(acc)

    e_b = jnp.exp(b_col)
    b_col3 = b_col.reshape(nch, CHUNK, LANES)
    e_lb = jnp.exp(b_col3[:, CHUNK - 1:CHUNK, :] - b_col3).reshape(tc, LANES)
    beta = betac_ref[...]
    cq, ck, cv = conv(0), conv(1), conv(2)
    for h in range(GDN_HEADS):
        cols = slice(h * GDN_DK, (h + 1) * GDN_DK)
        be_h = beta[:, BETA_LANE + h:BETA_LANE + h + 1]
        eb_h = e_b[:, AIN_LANE + h:AIN_LANE + h + 1]
        elb_h = e_lb[:, AIN_LANE + h:AIN_LANE + h + 1]
        qh = cq[:, cols]
        qh = mrow(qh * lax.rsqrt(jnp.sum(qh * qh, axis=-1, keepdims=True) + EPS) * (GDN_DK ** -0.5))
        kh_ = ck[:, cols]
        kh_ = mrow(kh_ * lax.rsqrt(jnp.sum(kh_ * kh_, axis=-1, keepdims=True) + EPS))
        vh = mrow(cv[:, cols])
        qn_ref[:, cols] = qh.astype(BF16)
        kn_ref[:, cols] = kh_.astype(BF16)
        qdec_ref[:, cols] = (qh * eb_h).astype(BF16)
        kdec_ref[:, cols] = (kh_ * elb_h).astype(BF16)
        bv_ref[:, cols] = (be_h * vh).astype(BF16)
        bek_ref[:, cols] = ((be_h * eb_h) * kh_).astype(BF16)

    tail = ubuf_ref[HIST + n_valid - (CONV_W - 1):HIST + n_valid, :]
    ubuf_ref[HIST - (CONV_W - 1):HIST, :] = tail
    conv_ref[0] = tail

    ri = lax.broadcasted_iota(jnp.int32, (CHUNK, CHUNK), 0)
    ci = lax.broadcasted_iota(jnp.int32, (CHUNK, CHUNK), 1)
    incl = ri >= ci
    strict = ri > ci
    eye = jnp.where(ri == ci, 1.0, 0.0).astype(F32)
    blk = {}
    s_ = INV_BASE
    while s_ <= CHUNK:
        sh = s_.bit_length() - 1
        blk[s_] = (ri >> sh) == (ci >> sh)
        s_ *= 2
    gna = gna_ref[...]
    gnb = gnb_ref[...]

    def chunk_body(c, carry):
        rows = pl.ds(pl.multiple_of(c * CHUNK, CHUNK), CHUNK)
        dec_a = bdec_ref[c]
        for h in range(GLA_HEADS):
            lk = slice(h * dk_a, (h + 1) * dk_a)
            lv = slice(h * dv_a, (h + 1) * dv_a)
            v = va_ref[rows, lv]
            att = jnp.where(incl, _dot_nt(qe_ref[rows, lk], ke_ref[rows, lk]), 0.0)
            st = sgla_ref[0, h]
            o = _dot(att.astype(BF16), v) + _dot_nt(qd_ref[rows, lk], st.astype(BF16))
            sgla_ref[0, h] = dec_a[:, lk] * st + _dot_tn(v, kd_ref[rows, lk])
            o = o * lax.rsqrt(jnp.mean(o * o, axis=-1, keepdims=True) + EPS) * gna
            oa_ref[rows, lv] = (o * sza_ref[rows, lv]).astype(BF16)
        bc_all = bcol_ref[rows, :]
        be_all = betac_ref[rows, :]
        br_all = brow_ref[c]
        dv_all = dvec_ref[c]
        for h in range(GDN_HEADS):
            cols = slice(h * GDN_DK, (h + 1) * GDN_DK)
            diff = bc_all[:, AIN_LANE + h:AIN_LANE + h + 1] - br_all[h:h + 1, :]
            dm = jnp.where(incl, jnp.exp(jnp.where(incl, diff, 0.0)), 0.0)
            k = kn_ref[rows, cols]
            q = qn_ref[rows, cols]
            kk = _dot_nt(k, k)
            qk = _dot_nt(q, k) * dm
            a = jnp.where(strict, be_all[:, BETA_LANE + h:BETA_LANE + h + 1] * kk * dm, 0.0)
            dblk = jnp.where(blk[INV_BASE], a, 0.0)
            tinv = eye - dblk
            p = dblk.astype(BF16)
            n = 2
            while n < INV_BASE:
                p = _dot(p, p).astype(BF16)
                tinv = tinv + _dot(tinv.astype(BF16), p)
                n *= 2
            while n < CHUNK:
                e = jnp.where(blk[2 * n] & ~blk[n], a, 0.0).astype(BF16)
                tb = tinv.astype(BF16)
                tinv = tinv - _dot(_dot(tb, e).astype(BF16), tb)
                n *= 2
            tb = tinv.astype(BF16)
            uv = _dot(tb, bv_ref[rows, cols])
            wk = _dot(tb, bek_ref[rows, cols])
            s = sgdn_ref[0, h]
            sb = s.astype(BF16)
            u = (uv - _dot(wk.astype(BF16), sb)).astype(BF16)
            o = _dot(qdec_ref[rows, cols], sb) + _dot(qk.astype(BF16), u)
            sgdn_ref[0, h] = dv_all[h:h + 1, :] * s + _dot_tn(kdec_ref[rows, cols], u)
            o = o * lax.rsqrt(jnp.mean(o * o, axis=-1, keepdims=True) + EPS) * gnb
            ob_ref[rows, cols] = (o * szb_ref[rows, cols]).astype(BF16)
        return carry

    lax.fori_loop(0, nch, chunk_body, 0)

    merged = sga_ref[...] * _dot(oa_ref[...], wpa_ref[...]) + sgb_ref[...] * _dot(ob_ref[...], wpb_ref[...])
    out = _dot(merged.astype(BF16), wout_ref[...])
    xn = x_ref[0] + mod_ref[0][:, 2 * d:3 * d] * out
    if final_norm:
        xn = xn * lax.rsqrt(jnp.mean(xn * xn, axis=-1, keepdims=True) + EPS) * gfin_ref[...]
    y_ref[0] = xn


def _resident(shape):
    zeros = (0,) * len(shape)
    return pl.BlockSpec(shape, lambda b, t: zeros, pipeline_mode=pl.Buffered(1))


def _layer(x, mod, consts, states, *, tc, n_valid, final_norm):
    bsz, t_len, d = x.shape
    nch = tc // CHUNK
    qk_a = d // 2
    has_state = states is not None
    const_specs = [_resident(c.shape) for c in consts]
    in_specs = [pl.BlockSpec((1, tc, d), lambda b, t: (b, t, 0)),
                pl.BlockSpec((1, 1, 3 * d), lambda b, t: (b, 0, 0))] + const_specs
    args = [x, mod.reshape(bsz, 1, 3 * d)] + list(consts)
    state_shapes = [(bsz, GLA_HEADS, d // GLA_HEADS, qk_a // GLA_HEADS),
                    (bsz, GDN_HEADS, GDN_DK, GDN_DV),
                    (bsz, CONV_W - 1, 3 * d)]
    state_specs = [pl.BlockSpec((1,) + s[1:], lambda b, t, n=len(s): (b,) + (0,) * (n - 1)) for s in state_shapes]
    if has_state:
        in_specs += state_specs
        args += list(states)
    scratch = [
        pltpu.VMEM((tc, d), BF16),
        pltpu.VMEM((tc, qk_a), BF16), pltpu.VMEM((tc, qk_a), BF16),
        pltpu.VMEM((tc, qk_a), BF16), pltpu.VMEM((tc, qk_a), BF16),
        pltpu.VMEM((tc, d), BF16),
        pltpu.VMEM((tc, d), F32),
        pltpu.VMEM((nch, 1, qk_a), F32),
        pltpu.VMEM((HIST + tc, 3 * d), F32),
        pltpu.VMEM((tc, d), BF16), pltpu.VMEM((tc, d), BF16),
        pltpu.VMEM((tc, d), BF16), pltpu.VMEM((tc, d), BF16),
        pltpu.VMEM((tc, d), BF16), pltpu.VMEM((tc, d), BF16),
        pltpu.VMEM((tc, d), F32),
        pltpu.VMEM((tc, d), F32), pltpu.VMEM((tc, d), F32),
        pltpu.VMEM((tc, LANES), F32), pltpu.VMEM((tc, LANES), F32),
        pltpu.VMEM((nch, GDN_HEADS, CHUNK), F32),
        pltpu.VMEM((nch, GDN_HEADS, LANES), F32),
        pltpu.VMEM((tc, d), BF16), pltpu.VMEM((tc, d), BF16),
    ]
    kern = functools.partial(_layer_kernel, tc=tc, n_valid=n_valid, has_state=has_state,
                             final_norm=final_norm, d_model=d)
    return pl.pallas_call(
        kern,
        grid=(bsz, t_len // tc),
        in_specs=in_specs,
        out_specs=[pl.BlockSpec((1, tc, d), lambda b, t: (b, t, 0))] + state_specs,
        out_shape=[jax.ShapeDtypeStruct(x.shape, F32)] + [jax.ShapeDtypeStruct(s, F32) for s in state_shapes],
        scratch_shapes=scratch,
        compiler_params=pltpu.CompilerParams(dimension_semantics=("arbitrary", "arbitrary"),
                                             vmem_limit_bytes=VMEM_LIMIT_BYTES),
        name="gla_gdn_layer",
    )(*args)


def _layer_consts(tc, g_norm1, w_in, w_gk2, b_gk, w_conv, a_log, dt_bias, g_norm_a, g_norm_b,
                  w_pa, w_pb, w_out, g_final):
    d = w_in.shape[0]
    qk_a = d // 2
    o_gk = 2 * qk_a + 2 * d
    o_qkv = o_gk + GLA_RANK
    o_zb = o_qkv + 3 * d
    o_beta = o_zb + d
    o_a = o_beta + GDN_HEADS
    o_ga = o_a + GDN_HEADS
    pad = LANES - (GLA_RANK + 2 * GDN_HEADS)
    w_perm = jnp.concatenate(
        [w_in[:, 0:o_gk], w_in[:, o_qkv:o_beta], w_in[:, o_ga:o_ga + 2 * d],
         w_in[:, o_gk:o_qkv], w_in[:, o_beta:o_ga], jnp.zeros((d, pad), w_in.dtype)], axis=1).astype(BF16)
    w_at = w_in[:, o_a:o_ga].T.astype(BF16)
    w_gk = jnp.zeros((LANES, qk_a), F32).at[0:GLA_RANK].set(w_gk2).astype(BF16)
    lane_vec = lambda v: jnp.zeros((1, LANES), F32).at[0, AIN_LANE:AIN_LANE + GDN_HEADS].set(v)
    col_vec = lambda v: jnp.broadcast_to(v.reshape(GDN_HEADS, 1), (GDN_HEADS, CHUNK)).astype(F32)
    tt = jnp.arange(tc)
    bd = ((tt[:, None] // CHUNK == tt[None, :] // CHUNK) & (tt[None, :] <= tt[:, None])).astype(BF16)
    tu = jnp.arange(CHUNK)
    ut = (tu[:, None] <= tu[None, :]).astype(BF16)
    return [g_norm1.reshape(1, d), w_perm, w_at, w_gk, b_gk.reshape(1, qk_a), w_conv,
            lane_vec(a_log), lane_vec(dt_bias), col_vec(a_log), col_vec(dt_bias),
            g_norm_a.reshape(1, -1), g_norm_b.reshape(1, -1),
            w_pa.astype(BF16), w_pb.astype(BF16), w_out.astype(BF16), g_final.reshape(1, d), bd, ut]


PROMPT_TILE = 256


def kernel(x_prompt, x_sample, c_prompt, c_sample, state_gla, state_gdn, cache_conv_gdn, w_ada, b_ada, g_norm1, w_in, w_gk2, b_gk, w_conv, a_log, dt_bias, g_norm_a, g_norm_b, w_pa, w_pb, w_out, g_final):
    depth = w_in.shape[0]
    bp, tp, _ = x_prompt.shape
    bs, ts, _ = x_sample.shape
    assert tp % PROMPT_TILE == 0 and CONV_W - 1 <= ts <= CHUNK
    hp = x_prompt
    hs = jnp.pad(x_sample, ((0, 0), (0, CHUNK - ts), (0, 0)))
    outs_p, outs_s = [], []
    for layer in range(depth):
        last = layer == depth - 1
        mod = _adaln_mod(jnp.concatenate([c_prompt, c_sample], axis=0), w_ada[layer], b_ada[layer])
        lw = (g_norm1[layer], w_in[layer], w_gk2[layer], b_gk[layer], w_conv[layer], a_log[layer],
              dt_bias[layer], g_norm_a[layer], g_norm_b[layer], w_pa[layer], w_pb[layer], w_out[layer], g_final)
        hp, *st_p = _layer(hp, mod[:bp], _layer_consts(PROMPT_TILE, *lw), None,
                           tc=PROMPT_TILE, n_valid=PROMPT_TILE, final_norm=last)
        st_in = (jnp.swapaxes(state_gla[layer], -1, -2), state_gdn[layer], cache_conv_gdn[layer])
        hs, *st_s = _layer(hs, mod[bp:], _layer_consts(CHUNK, *lw), st_in,
                           tc=CHUNK, n_valid=ts, final_norm=last)
        outs_p.append(st_p)
        outs_s.append(st_s)

    def stack(outs, i, transpose=False):
        a = jnp.stack([o[i] for o in outs])
        return jnp.swapaxes(a, -1, -2) if transpose else a

    return (hp, hs[:, :ts], stack(outs_p, 0, True), stack(outs_p, 1), stack(outs_p, 2),
            stack(outs_s, 0, True), stack(outs_s, 1), stack(outs_s, 2))
```

```python
import functools

import jax
import jax.numpy as jnp
from jax import lax
from jax.experimental import pallas as pl
from jax.experimental.pallas import tpu as pltpu

F32 = jnp.float32
BF16 = jnp.bfloat16

CHUNK = 64
EPS = 1e-6
GLA_HEADS = 4
GLA_RANK = 16
GLA_GATE_NORM = 16.0
GDN_HEADS = 8
GDN_DK = 128
GDN_DV = 128
CONV_W = 4
LANES = 128
HIST = 8
INV_BASE = 8
WY_GROUP = 2
BETA_LANE = GLA_RANK
AIN_LANE = GLA_RANK + GDN_HEADS
VMEM_LIMIT_BYTES = 56 * 1024 * 1024


def _dot(a, b):
    return jnp.dot(a, b, preferred_element_type=F32)


def _dot_nt(a, b):
    return lax.dot_general(a, b, (((1,), (1,)), ((), ())), preferred_element_type=F32)


def _dot_tn(a, b):
    return lax.dot_general(a, b, (((0,), (0,)), ((), ())), preferred_element_type=F32)


def _split3(x):
    hi = x.astype(BF16)
    r = x - hi.astype(F32)
    mid = r.astype(BF16)
    lo = (r - mid.astype(F32)).astype(BF16)
    return hi, mid, lo


def _softplus(x):
    return jnp.maximum(x, 0.0) + jnp.log1p(jnp.exp(-jnp.abs(x)))


def _log_sigmoid(x):
    return jnp.minimum(x, 0.0) - jnp.log1p(jnp.exp(-jnp.abs(x)))


def _silu(x):
    return x * jax.nn.sigmoid(x)


def _mod_kernel(c_ref, w_ref, b_ref, o_ref):
    s = _silu(c_ref[...]).astype(BF16)
    o_ref[...] = _dot(s, w_ref[...].astype(BF16)) + b_ref[...]


def _adaln_mod(c, w_ada, b_ada):
    n, d = c.shape
    d3 = w_ada.shape[1]
    bn = 512
    return pl.pallas_call(
        _mod_kernel,
        grid=(d3 // bn,),
        in_specs=[pl.BlockSpec((n, d), lambda j: (0, 0)),
                  pl.BlockSpec((d, bn), lambda j: (0, j)),
                  pl.BlockSpec((1, bn), lambda j: (0, j))],
        out_specs=pl.BlockSpec((n, bn), lambda j: (0, j)),
        out_shape=jax.ShapeDtypeStruct((n, d3), F32),
        name="adaln_mod",
    )(c, w_ada, b_ada.reshape(1, d3))


def _layer_kernel(*refs, tc, n_valid, has_state, final_norm, d_model):
    d = d_model
    qk_a = d // 2
    dk_a = qk_a // GLA_HEADS
    dv_a = d // GLA_HEADS
    nch = tc // CHUNK
    o_qa, o_ka, o_va, o_za = 0, qk_a, 2 * qk_a, 2 * qk_a + d
    o_qb = o_za + d
    o_kb, o_vb = o_qb + d, o_qb + 2 * d
    o_zb = o_qb + 3 * d
    o_ga, o_gb = o_zb + d, o_zb + 2 * d
    o_sm = o_gb + d

    it = iter(refs)
    x_ref, mod_ref, g1_ref, win_ref, wat_ref, wgk_ref, bgk_ref, wconv_ref = (next(it) for _ in range(8))
    alane_ref, dlane_ref, acol_ref, dcol_ref = (next(it) for _ in range(4))
    gna_ref, gnb_ref, wpa_ref, wpb_ref, wout_ref, gfin_ref, bd_ref, ut_ref = (next(it) for _ in range(8))
    if has_state:
        sgla_in, sgdn_in, conv_in = (next(it) for _ in range(3))
    y_ref, sgla_ref, sgdn_ref, conv_ref = (next(it) for _ in range(4))
    (hb_ref, qe_ref, ke_ref, qd_ref, kd_ref, va_ref, sza_ref, bdec_ref, ubuf_ref,
     qn_ref, kn_ref, qdec_ref, kdec_ref, bv_ref, bek_ref, szb_ref,
     bcol_ref, betac_ref, brow_ref, dvec_ref, uv_ref, wk_ref, qkm_ref, oa_ref, ob_ref) = (next(it) for _ in range(25))

    t = pl.program_id(1)

    @pl.when(t == 0)
    def _init():
        if has_state:
            sgla_ref[...] = sgla_in[...]
            sgdn_ref[...] = sgdn_in[...]
            ubuf_ref[HIST - (CONV_W - 1):HIST, :] = conv_in[0]
        else:
            sgla_ref[...] = jnp.zeros_like(sgla_ref)
            sgdn_ref[...] = jnp.zeros_like(sgdn_ref)
            ubuf_ref[0:HIST, :] = jnp.zeros((HIST, ubuf_ref.shape[1]), F32)

    masked = n_valid < tc
    if masked:
        rowmask = lax.broadcasted_iota(jnp.int32, (tc, 1), 0) < n_valid

    def mrow(v):
        return jnp.where(rowmask, v, 0.0) if masked else v

    x = x_ref[0]
    mod = mod_ref[0]
    shift, scale = mod[:, 0:d], mod[:, d:2 * d]
    hn = x * lax.rsqrt(jnp.mean(x * x, axis=-1, keepdims=True) + EPS) * g1_ref[...]
    hb_ref[...] = (hn * (1.0 + scale) + shift).astype(BF16)

    def proj(c0, width):
        return _dot(hb_ref[...], win_ref[:, c0:c0 + width])

    ps = proj(o_sm, LANES)
    lane = lax.broadcasted_iota(jnp.int32, (1, LANES), 1)
    betac_ref[...] = mrow(jax.nn.sigmoid(ps))
    g_col = -jnp.exp(alane_ref[...]) * _softplus(ps + dlane_ref[...])
    g_col = mrow(jnp.where((lane >= AIN_LANE) & (lane < AIN_LANE + GDN_HEADS), g_col, 0.0))
    bd = bd_ref[...]
    gh, gm, gl = _split3(g_col)
    b_col = _dot(bd, gh) + _dot(bd, gm) + _dot(bd, gl)
    bcol_ref[...] = b_col

    ut = ut_ref[...]
    for c in range(nch):
        arow = _dot_nt(wat_ref[...], hb_ref[c * CHUNK:(c + 1) * CHUNK, :])
        g_row = -jnp.exp(acol_ref[...]) * _softplus(arow + dcol_ref[...])
        if masked:
            colmask = (lax.broadcasted_iota(jnp.int32, (1, CHUNK), 1) + c * CHUNK) < n_valid
            g_row = jnp.where(colmask, g_row, 0.0)
        rh, rm, rl = _split3(g_row)
        b_row = _dot(rh, ut) + _dot(rm, ut) + _dot(rl, ut)
        brow_ref[c] = b_row
        dvec_ref[c] = jnp.broadcast_to(jnp.exp(b_row[:, CHUNK - 1:CHUNK]), (GDN_HEADS, LANES))

    gk = _log_sigmoid(_dot(ps.astype(BF16), wgk_ref[...]) + bgk_ref[...]) * (1.0 / GLA_GATE_NORM)
    gk = mrow(gk)
    kh, km, kl = _split3(gk)
    b_a = (_dot(bd, kh) + _dot(bd, km) + _dot(bd, kl)).reshape(nch, CHUNK, qk_a)
    b_mid = b_a[:, CHUNK // 2 - 1:CHUNK // 2, :]
    b_last = b_a[:, CHUNK - 1:CHUNK, :]
    bdec_ref[...] = jnp.exp(b_last)

    pqk = proj(o_qa, 2 * qk_a)
    qa = mrow(pqk[:, 0:qk_a] * (dk_a ** -0.5)).reshape(nch, CHUNK, qk_a)
    ka = mrow(pqk[:, qk_a:2 * qk_a]).reshape(nch, CHUNK, qk_a)
    qe_ref[...] = (qa * jnp.exp(b_a - b_mid)).reshape(tc, qk_a).astype(BF16)
    ke_ref[...] = (ka * jnp.exp(b_mid - b_a)).reshape(tc, qk_a).astype(BF16)
    qd_ref[...] = (qa * jnp.exp(b_a)).reshape(tc, qk_a).astype(BF16)
    kd_ref[...] = (ka * jnp.exp(b_last - b_a)).reshape(tc, qk_a).astype(BF16)

    va_ref[...] = mrow(proj(o_va, d)).astype(BF16)
    sza_ref[...] = _silu(proj(o_za, d))
    szb_ref[...] = _silu(proj(o_zb, d))

    for j in range(3):
        ubuf_ref[HIST:HIST + tc, j * d:(j + 1) * d] = proj(o_qb + j * d, d)

    def conv(j):
        cols = slice(j * d, (j + 1) * d)
        acc = ubuf_ref[HIST - (CONV_W - 1):HIST - (CONV_W - 1) + tc, cols] * wconv_ref[0:1, cols]
        for i in range(1, CONV_W):
            s = HIST - (CONV_W - 1) + i
            acc = acc + ubuf_ref[s:s + tc, cols] * wconv_ref[i:i + 1, cols]
        return _silu(acc)

    e_b = jnp.exp(b_col)
    b_col3 = b_col.reshape(nch, CHUNK, LANES)
    e_lb = jnp.exp(b_col3[:, CHUNK - 1:CHUNK, :] - b_col3).reshape(tc, LANES)
    beta = betac_ref[...]
    cq, ck, cv = conv(0), conv(1), conv(2)
    for h in range(GDN_HEADS):
        cols = slice(h * GDN_DK, (h + 1) * GDN_DK)
        be_h = beta[:, BETA_LANE + h:BETA_LANE + h + 1]
        eb_h = e_b[:, AIN_LANE + h:AIN_LANE + h + 1]
        elb_h = e_lb[:, AIN_LANE + h:AIN_LANE + h + 1]
        qh = cq[:, cols]
        qh = mrow(qh * lax.rsqrt(jnp.sum(qh * qh, axis=-1, keepdims=True) + EPS) * (GDN_DK ** -0.5))
        kh_ = ck[:, cols]
        kh_ = mrow(kh_ * lax.rsqrt(jnp.sum(kh_ * kh_, axis=-1, keepdims=True) + EPS))
        vh = mrow(cv[:, cols])
        qn_ref[:, cols] = qh.astype(BF16)
        kn_ref[:, cols] = kh_.astype(BF16)
        qdec_ref[:, cols] = (qh * eb_h).astype(BF16)
        kdec_ref[:, cols] = (kh_ * elb_h).astype(BF16)
        bv_ref[:, cols] = (be_h * vh).astype(BF16)
        bek_ref[:, cols] = ((be_h * eb_h) * kh_).astype(BF16)

    tail = ubuf_ref[HIST + n_valid - (CONV_W - 1):HIST + n_valid, :]
    ubuf_ref[HIST - (CONV_W - 1):HIST, :] = tail
    conv_ref[0] = tail

    ri = lax.broadcasted_iota(jnp.int32, (CHUNK, CHUNK), 0)
    ci = lax.broadcasted_iota(jnp.int32, (CHUNK, CHUNK), 1)
    incl = ri >= ci
    strict = ri > ci
    eye = jnp.where(ri == ci, 1.0, 0.0).astype(F32)
    blk = {}
    s_ = INV_BASE
    while s_ <= CHUNK:
        sh = s_.bit_length() - 1
        blk[s_] = (ri >> sh) == (ci >> sh)
        s_ *= 2
    gna = gna_ref[...]
    gnb = gnb_ref[...]

    def chunk_rows(c):
        return pl.ds(pl.multiple_of(c * CHUNK, CHUNK), CHUNK)

    group = min(WY_GROUP, nch)

    def wy_body(i, carry):
        chains = [(g, h) for g in range(group) for h in range(GDN_HEADS)]
        cidx = [i * group + g for g in range(group)]
        crow = [chunk_rows(c) for c in cidx]
        hcols = lambda h: slice(h * GDN_DK, (h + 1) * GDN_DK)
        ks = [kn_ref[crow[g], hcols(h)] for g, h in chains]
        qs = [qn_ref[crow[g], hcols(h)] for g, h in chains]
        bcs = [bcol_ref[crow[g], :] for g in range(group)]
        bes = [betac_ref[crow[g], :] for g in range(group)]
        brs = [brow_ref[cidx[g]] for g in range(group)]
        kks = [_dot_nt(k, k) for k in ks]
        qks = [_dot_nt(q, k) for q, k in zip(qs, ks)]
        a_s, tinvs, ps_ = [], [], []
        for n_, (g, h) in enumerate(chains):
            diff = bcs[g][:, AIN_LANE + h:AIN_LANE + h + 1] - brs[g][h:h + 1, :]
            dm = jnp.where(incl, jnp.exp(jnp.where(incl, diff, 0.0)), 0.0)
            qkm_ref[cidx[g], h] = (qks[n_] * dm).astype(BF16)
            a = jnp.where(strict, bes[g][:, BETA_LANE + h:BETA_LANE + h + 1] * kks[n_] * dm, 0.0)
            dblk = jnp.where(blk[INV_BASE], a, 0.0)
            a_s.append(a)
            tinvs.append(eye - dblk)
            ps_.append(dblk.astype(BF16))
        n = 2
        while n < INV_BASE:
            ps_ = [_dot(p, p).astype(BF16) for p in ps_]
            tinvs = [t_ + _dot(t_.astype(BF16), p) for t_, p in zip(tinvs, ps_)]
            n *= 2
        while n < CHUNK:
            es = [jnp.where(blk[2 * n] & ~blk[n], a, 0.0).astype(BF16) for a in a_s]
            tbs = [t_.astype(BF16) for t_ in tinvs]
            tes = [_dot(tb, e).astype(BF16) for tb, e in zip(tbs, es)]
            tinvs = [t_ - _dot(te, tb) for t_, te, tb in zip(tinvs, tes, tbs)]
            n *= 2
        for t_, (g, h) in zip(tinvs, chains):
            tb = t_.astype(BF16)
            uv_ref[crow[g], hcols(h)] = _dot(tb, bv_ref[crow[g], hcols(h)])
            wk_ref[crow[g], hcols(h)] = _dot(tb, bek_ref[crow[g], hcols(h)]).astype(BF16)
        return carry

    lax.fori_loop(0, nch // group, wy_body, 0)

    def chunk_body(c, carry):
        rows = chunk_rows(c)
        ha, hb_ = range(GLA_HEADS), range(GDN_HEADS)
        lk = lambda h: slice(h * dk_a, (h + 1) * dk_a)
        lv = lambda h: slice(h * dv_a, (h + 1) * dv_a)
        hc = lambda h: slice(h * GDN_DK, (h + 1) * GDN_DK)
        sa = [sgla_ref[0, h] for h in ha]
        sb = [sgdn_ref[0, h] for h in hb_]
        v_a = [va_ref[rows, lv(h)] for h in ha]
        dec_a = bdec_ref[c]
        dv_all = dvec_ref[c]
        sab = [s_.astype(BF16) for s_ in sa]
        sbb = [s_.astype(BF16) for s_ in sb]
        ws = [_dot(wk_ref[rows, hc(h)], sbb[h]) for h in hb_]
        qsb = [_dot(qdec_ref[rows, hc(h)], sbb[h]) for h in hb_]
        att = [jnp.where(incl, _dot_nt(qe_ref[rows, lk(h)], ke_ref[rows, lk(h)]), 0.0).astype(BF16) for h in ha]
        oi = [_dot(qd_ref[rows, lk(h)], sab[h]) for h in ha]
        kv = [_dot_tn(kd_ref[rows, lk(h)], v_a[h]) for h in ha]
        u = [(uv_ref[rows, hc(h)] - ws[h]).astype(BF16) for h in hb_]
        o_b = [qsb[h] + _dot(qkm_ref[c, h], u[h]) for h in hb_]
        sb_new = [dv_all[h:h + 1, :] * sb[h] + _dot_tn(kdec_ref[rows, hc(h)], u[h]) for h in hb_]
        o_a = [_dot(att[h], v_a[h]) + oi[h] for h in ha]
        sa_new = []
        for h in ha:
            dcol = jnp.broadcast_to(dec_a[:, lk(h)], (dk_a, dk_a)).T
            sa_new.append(jnp.concatenate([dcol] * (dv_a // dk_a), axis=1) * sa[h] + kv[h])
        for h in ha:
            o = o_a[h]
            o = o * lax.rsqrt(jnp.mean(o * o, axis=-1, keepdims=True) + EPS) * gna
            oa_ref[rows, lv(h)] = (o * sza_ref[rows, lv(h)]).astype(BF16)
            sgla_ref[0, h] = sa_new[h]
        for h in hb_:
            o = o_b[h]
            o = o * lax.rsqrt(jnp.mean(o * o, axis=-1, keepdims=True) + EPS) * gnb
            ob_ref[rows, hc(h)] = (o * szb_ref[rows, hc(h)]).astype(BF16)
            sgdn_ref[0, h] = sb_new[h]
        return carry

    lax.fori_loop(0, nch, chunk_body, 0)

    merged = (jax.nn.sigmoid(proj(o_ga, d)) * _dot(oa_ref[...], wpa_ref[...])
              + jax.nn.sigmoid(proj(o_gb, d)) * _dot(ob_ref[...], wpb_ref[...]))
    out = _dot(merged.astype(BF16), wout_ref[...])
    xn = x_ref[0] + mod_ref[0][:, 2 * d:3 * d] * out
    if final_norm:
        xn = xn * lax.rsqrt(jnp.mean(xn * xn, axis=-1, keepdims=True) + EPS) * gfin_ref[...]
    y_ref[0] = xn


def _resident(shape):
    zeros = (0,) * len(shape)
    return pl.BlockSpec(shape, lambda b, t: zeros, pipeline_mode=pl.Buffered(1))


def _layer(x, mod, consts, states, *, tc, n_valid, final_norm):
    bsz, t_len, d = x.shape
    nch = tc // CHUNK
    qk_a = d // 2
    has_state = states is not None
    const_specs = [_resident(c.shape) for c in consts]
    in_specs = [pl.BlockSpec((1, tc, d), lambda b, t: (b, t, 0)),
                pl.BlockSpec((1, 1, 3 * d), lambda b, t: (b, 0, 0))] + const_specs
    args = [x, mod.reshape(bsz, 1, 3 * d)] + list(consts)
    state_shapes = [(bsz, GLA_HEADS, qk_a // GLA_HEADS, d // GLA_HEADS),
                    (bsz, GDN_HEADS, GDN_DK, GDN_DV),
                    (bsz, CONV_W - 1, 3 * d)]
    state_specs = [pl.BlockSpec((1,) + s[1:], lambda b, t, n=len(s): (b,) + (0,) * (n - 1)) for s in state_shapes]
    if has_state:
        in_specs += state_specs
        args += list(states)
    scratch = [
        pltpu.VMEM((tc, d), BF16),
        pltpu.VMEM((tc, qk_a), BF16), pltpu.VMEM((tc, qk_a), BF16),
        pltpu.VMEM((tc, qk_a), BF16), pltpu.VMEM((tc, qk_a), BF16),
        pltpu.VMEM((tc, d), BF16),
        pltpu.VMEM((tc, d), F32),
        pltpu.VMEM((nch, 1, qk_a), F32),
        pltpu.VMEM((HIST + tc, 3 * d), F32),
        pltpu.VMEM((tc, d), BF16), pltpu.VMEM((tc, d), BF16),
        pltpu.VMEM((tc, d), BF16), pltpu.VMEM((tc, d), BF16),
        pltpu.VMEM((tc, d), BF16), pltpu.VMEM((tc, d), BF16),
        pltpu.VMEM((tc, d), F32),
        pltpu.VMEM((tc, LANES), F32), pltpu.VMEM((tc, LANES), F32),
        pltpu.VMEM((nch, GDN_HEADS, CHUNK), F32),
        pltpu.VMEM((nch, GDN_HEADS, LANES), F32),
        pltpu.VMEM((tc, d), F32), pltpu.VMEM((tc, d), BF16),
        pltpu.VMEM((nch, GDN_HEADS, CHUNK, CHUNK), BF16),
        pltpu.VMEM((tc, d), BF16), pltpu.VMEM((tc, d), BF16),
    ]
    kern = functools.partial(_layer_kernel, tc=tc, n_valid=n_valid, has_state=has_state,
                             final_norm=final_norm, d_model=d)
    return pl.pallas_call(
        kern,
        grid=(bsz, t_len // tc),
        in_specs=in_specs,
        out_specs=[pl.BlockSpec((1, tc, d), lambda b, t: (b, t, 0))] + state_specs,
        out_shape=[jax.ShapeDtypeStruct(x.shape, F32)] + [jax.ShapeDtypeStruct(s, F32) for s in state_shapes],
        scratch_shapes=scratch,
        compiler_params=pltpu.CompilerParams(dimension_semantics=("arbitrary", "arbitrary"),
                                             vmem_limit_bytes=VMEM_LIMIT_BYTES),
        name="gla_gdn_layer",
    )(*args)


def _layer_consts(tc, g_norm1, w_in, w_gk2, b_gk, w_conv, a_log, dt_bias, g_norm_a, g_norm_b,
                  w_pa, w_pb, w_out, g_final):
    d = w_in.shape[0]
    qk_a = d // 2
    o_gk = 2 * qk_a + 2 * d
    o_qkv = o_gk + GLA_RANK
    o_zb = o_qkv + 3 * d
    o_beta = o_zb + d
    o_a = o_beta + GDN_HEADS
    o_ga = o_a + GDN_HEADS
    pad = LANES - (GLA_RANK + 2 * GDN_HEADS)
    w_perm = jnp.concatenate(
        [w_in[:, 0:o_gk], w_in[:, o_qkv:o_beta], w_in[:, o_ga:o_ga + 2 * d],
         w_in[:, o_gk:o_qkv], w_in[:, o_beta:o_ga], jnp.zeros((d, pad), w_in.dtype)], axis=1).astype(BF16)
    w_at = w_in[:, o_a:o_ga].T.astype(BF16)
    w_gk = jnp.zeros((LANES, qk_a), F32).at[0:GLA_RANK].set(w_gk2).astype(BF16)
    lane_vec = lambda v: jnp.zeros((1, LANES), F32).at[0, AIN_LANE:AIN_LANE + GDN_HEADS].set(v)
    col_vec = lambda v: jnp.broadcast_to(v.reshape(GDN_HEADS, 1), (GDN_HEADS, CHUNK)).astype(F32)
    tt = jnp.arange(tc)
    bd = ((tt[:, None] // CHUNK == tt[None, :] // CHUNK) & (tt[None, :] <= tt[:, None])).astype(BF16)
    tu = jnp.arange(CHUNK)
    ut = (tu[:, None] <= tu[None, :]).astype(BF16)
    return [g_norm1.reshape(1, d), w_perm, w_at, w_gk, b_gk.reshape(1, qk_a), w_conv,
            lane_vec(a_log), lane_vec(dt_bias), col_vec(a_log), col_vec(dt_bias),
            g_norm_a.reshape(1, -1), g_norm_b.reshape(1, -1),
            w_pa.astype(BF16), w_pb.astype(BF16), w_out.astype(BF16), g_final.reshape(1, d), bd, ut]


PROMPT_TILE = 256


def kernel(x_prompt, x_sample, c_prompt, c_sample, state_gla, state_gdn, cache_conv_gdn, w_ada, b_ada, g_norm1, w_in, w_gk2, b_gk, w_conv, a_log, dt_bias, g_norm_a, g_norm_b, w_pa, w_pb, w_out, g_final):
    depth = w_in.shape[0]
    bp, tp, _ = x_prompt.shape
    bs, ts, _ = x_sample.shape
    assert tp % PROMPT_TILE == 0 and CONV_W - 1 <= ts <= CHUNK
    hp = x_prompt
    hs = jnp.pad(x_sample, ((0, 0), (0, CHUNK - ts), (0, 0)))
    outs_p, outs_s = [], []
    for layer in range(depth):
        last = layer == depth - 1
        mod = _adaln_mod(jnp.concatenate([c_prompt, c_sample], axis=0), w_ada[layer], b_ada[layer])
        lw = (g_norm1[layer], w_in[layer], w_gk2[layer], b_gk[layer], w_conv[layer], a_log[layer],
              dt_bias[layer], g_norm_a[layer], g_norm_b[layer], w_pa[layer], w_pb[layer], w_out[layer], g_final)
        hp, *st_p = _layer(hp, mod[:bp], _layer_consts(PROMPT_TILE, *lw), None,
                           tc=PROMPT_TILE, n_valid=PROMPT_TILE, final_norm=last)
        st_in = (state_gla[layer], state_gdn[layer], cache_conv_gdn[layer])
        hs, *st_s = _layer(hs, mod[bp:], _layer_consts(CHUNK, *lw), st_in,
                           tc=CHUNK, n_valid=ts, final_norm=last)
        outs_p.append(st_p)
        outs_s.append(st_s)

    stack = lambda outs, i: jnp.stack([o[i] for o in outs])
    return (hp, hs[:, :ts], stack(outs_p, 0), stack(outs_p, 1), stack(outs_p, 2),
            stack(outs_s, 0), stack(outs_s, 1), stack(outs_s, 2))
```

```python
import functools

import jax
import jax.numpy as jnp
from jax import lax
from jax.experimental import pallas as pl
from jax.experimental.pallas import tpu as pltpu

F32 = jnp.float32
BF16 = jnp.bfloat16

CHUNK = 64
EPS = 1e-6
GLA_HEADS = 4
GLA_RANK = 16
GLA_GATE_NORM = 16.0
GDN_HEADS = 8
GDN_DK = 128
GDN_DV = 128
CONV_W = 4
LANES = 128
HIST = 8
INV_BASE = 8
SIDE_PER_PAIR = 2
BETA_LANE = GLA_RANK
AIN_LANE = GLA_RANK + GDN_HEADS
VMEM_LIMIT_BYTES = 56 * 1024 * 1024


def _dot(a, b):
    return jnp.dot(a, b, preferred_element_type=F32)


def _dot_nt(a, b):
    return lax.dot_general(a, b, (((1,), (1,)), ((), ())), preferred_element_type=F32)


def _dot_tn(a, b):
    return lax.dot_general(a, b, (((0,), (0,)), ((), ())), preferred_element_type=F32)


def _split2(x):
    hi = x.astype(BF16)
    return hi, (x - hi.astype(F32)).astype(BF16)


def _split3(x):
    hi = x.astype(BF16)
    r = x - hi.astype(F32)
    mid = r.astype(BF16)
    lo = (r - mid.astype(F32)).astype(BF16)
    return hi, mid, lo


def _softplus(x):
    return jnp.maximum(x, 0.0) + jnp.log1p(jnp.exp(-jnp.abs(x)))


def _log_sigmoid(x):
    return jnp.minimum(x, 0.0) - jnp.log1p(jnp.exp(-jnp.abs(x)))


def _silu(x):
    return x * jax.nn.sigmoid(x)


def _mod_kernel(c_ref, w_ref, b_ref, o_ref):
    s = _silu(c_ref[...]).astype(BF16)
    o_ref[...] = _dot(s, w_ref[...].astype(BF16)) + b_ref[...]


def _adaln_mod(c, w_ada, b_ada):
    n, d = c.shape
    d3 = w_ada.shape[1]
    bn = 512
    return pl.pallas_call(
        _mod_kernel,
        grid=(d3 // bn,),
        in_specs=[pl.BlockSpec((n, d), lambda j: (0, 0)),
                  pl.BlockSpec((d, bn), lambda j: (0, j)),
                  pl.BlockSpec((1, bn), lambda j: (0, j))],
        out_specs=pl.BlockSpec((n, bn), lambda j: (0, j)),
        out_shape=jax.ShapeDtypeStruct((n, d3), F32),
        name="adaln_mod",
    )(c, w_ada, b_ada.reshape(1, d3))


def _layer_kernel(*refs, tc, n_valid, has_state, final_norm, d_model):
    d = d_model
    qk_a = d // 2
    dk_a = qk_a // GLA_HEADS
    dv_a = d // GLA_HEADS
    nch = tc // CHUNK
    o_qa, o_ka, o_va, o_za = 0, qk_a, 2 * qk_a, 2 * qk_a + d
    o_qb = o_za + d
    o_kb, o_vb = o_qb + d, o_qb + 2 * d
    o_zb = o_qb + 3 * d
    o_ga, o_gb = o_zb + d, o_zb + 2 * d
    o_sm = o_gb + d

    it = iter(refs)
    x_ref, mod_ref, g1_ref, win_ref, wat_ref, wgk_ref, bgk_ref, wconv_ref = (next(it) for _ in range(8))
    alane_ref, dlane_ref, acol_ref, dcol_ref = (next(it) for _ in range(4))
    gna_ref, gnb_ref, wpa_ref, wpb_ref, wout_ref, gfin_ref, bd_ref, ut_ref = (next(it) for _ in range(8))
    if has_state:
        sgla_in, sgdn_in, conv_in = (next(it) for _ in range(3))
    y_ref, sgla_ref, sgdn_ref, conv_ref = (next(it) for _ in range(4))
    (hb_ref, qe_ref, ke_ref, qd_ref, kd_ref, va_ref, sza_ref, bdec_ref, ubuf_ref,
     qn_ref, kn_ref, qdec_ref, kdec_ref, bv_ref, bek_ref, szb_ref,
     bcol_ref, betac_ref, brow_ref, dvec_ref, uv_ref, wk_ref, qkm_ref, ba_ref, oa_ref, ob_ref) = (next(it) for _ in range(26))

    t = pl.program_id(1)

    @pl.when(t == 0)
    def _init():
        if has_state:
            sgla_ref[...] = sgla_in[...]
            sgdn_ref[...] = sgdn_in[...]
            ubuf_ref[HIST - (CONV_W - 1):HIST, :] = conv_in[0]
        else:
            sgla_ref[...] = jnp.zeros_like(sgla_ref)
            sgdn_ref[...] = jnp.zeros_like(sgdn_ref)
            ubuf_ref[0:HIST, :] = jnp.zeros((HIST, ubuf_ref.shape[1]), F32)

    masked = n_valid < tc
    if masked:
        rowmask = lax.broadcasted_iota(jnp.int32, (tc, 1), 0) < n_valid

    def mrow(v):
        return jnp.where(rowmask, v, 0.0) if masked else v

    x = x_ref[0]
    mod = mod_ref[0]
    shift, scale = mod[:, 0:d], mod[:, d:2 * d]
    hn = x * lax.rsqrt(jnp.mean(x * x, axis=-1, keepdims=True) + EPS) * g1_ref[...]
    hb_ref[...] = (hn * (1.0 + scale) + shift).astype(BF16)

    def proj(c0, width):
        return _dot(hb_ref[...], win_ref[:, c0:c0 + width])

    pair_w = 2 * GDN_DK
    half = d // 2

    def gdn_proj(jp, parts=(0, 1, 2)):
        for j in parts:
            c0 = j * d + jp * pair_w
            ubuf_ref[HIST:HIST + tc, c0:c0 + pair_w] = proj(o_qb + c0, pair_w)

    def va_task(i):
        va_ref[:, i * half:(i + 1) * half] = mrow(proj(o_va + i * half, half)).astype(BF16)

    def silu_task(dst_ref, c0, i):
        dst_ref[:, i * half:(i + 1) * half] = _silu(proj(c0 + i * half, half))

    ps = proj(o_sm, LANES)
    arows = [_dot_nt(wat_ref[...], hb_ref[c * CHUNK:(c + 1) * CHUNK, :]) for c in range(nch)]
    gdn_proj(0, (0,))
    lane = lax.broadcasted_iota(jnp.int32, (1, LANES), 1)
    betac_ref[...] = mrow(jax.nn.sigmoid(ps))
    g_col = -jnp.exp(alane_ref[...]) * _softplus(ps + dlane_ref[...])
    g_col = mrow(jnp.where((lane >= AIN_LANE) & (lane < AIN_LANE + GDN_HEADS), g_col, 0.0))
    g_rows = []
    for c in range(nch):
        g_row = -jnp.exp(acol_ref[...]) * _softplus(arows[c] + dcol_ref[...])
        if masked:
            colmask = (lax.broadcasted_iota(jnp.int32, (1, CHUNK), 1) + c * CHUNK) < n_valid
            g_row = jnp.where(colmask, g_row, 0.0)
        g_rows.append(g_row)
    gdn_proj(0, (1,))
    bd = bd_ref[...]
    ut = ut_ref[...]
    gh, gm, gl = _split3(g_col)
    b_col = _dot(bd, gh) + _dot(bd, gm) + _dot(bd, gl)
    b_rows = []
    for c in range(nch):
        rh, rm, rl = _split3(g_rows[c])
        b_rows.append(_dot(rh, ut) + _dot(rm, ut) + _dot(rl, ut))
    gdn_proj(0, (2,))
    va_task(0)
    bcol_ref[...] = b_col
    for c in range(nch):
        brow_ref[c] = b_rows[c]
        dvec_ref[c] = jnp.broadcast_to(jnp.exp(b_rows[c][:, CHUNK - 1:CHUNK]), (GDN_HEADS, LANES))
    e_b = jnp.exp(b_col)
    b_col3 = b_col.reshape(nch, CHUNK, LANES)
    e_lb = jnp.exp(b_col3[:, CHUNK - 1:CHUNK, :] - b_col3).reshape(tc, LANES)
    beta = betac_ref[...]
    va_task(1)

    def conv(c0, width):
        cols = slice(c0, c0 + width)
        full = ubuf_ref[:, cols]
        acc = full[HIST:] * wconv_ref[CONV_W - 1:CONV_W, cols]
        for i in range(CONV_W - 1):
            back = CONV_W - 1 - i
            acc = acc + pltpu.roll(full, back, 0)[HIST:] * wconv_ref[i:i + 1, cols]
        return _silu(acc)

    def gdn_prep(jp):
        cq, ck, cv = (conv(j * d + jp * pair_w, pair_w) for j in range(3))
        for hh in range(2):
            h = 2 * jp + hh
            cols = slice(h * GDN_DK, (h + 1) * GDN_DK)
            loc = slice(hh * GDN_DK, (hh + 1) * GDN_DK)
            be_h = beta[:, BETA_LANE + h:BETA_LANE + h + 1]
            eb_h = e_b[:, AIN_LANE + h:AIN_LANE + h + 1]
            elb_h = e_lb[:, AIN_LANE + h:AIN_LANE + h + 1]
            qh = cq[:, loc]
            qh = mrow(qh * lax.rsqrt(jnp.sum(qh * qh, axis=-1, keepdims=True) + EPS) * (GDN_DK ** -0.5))
            kh_ = ck[:, loc]
            kh_ = mrow(kh_ * lax.rsqrt(jnp.sum(kh_ * kh_, axis=-1, keepdims=True) + EPS))
            vh = mrow(cv[:, loc])
            qn_ref[:, cols] = qh.astype(BF16)
            kn_ref[:, cols] = kh_.astype(BF16)
            qdec_ref[:, cols] = (qh * eb_h).astype(BF16)
            kdec_ref[:, cols] = (kh_ * elb_h).astype(BF16)
            bv_ref[:, cols] = (be_h * vh).astype(BF16)
            bek_ref[:, cols] = ((be_h * eb_h) * kh_).astype(BF16)

    def gla_gate_task():
        gk = _log_sigmoid(_dot(ps.astype(BF16), wgk_ref[...]) + bgk_ref[...]) * (1.0 / GLA_GATE_NORM)
        gk = mrow(gk)
        kh, kl = _split2(gk)
        ba_ref[...] = _dot(bd, kh) + _dot(bd, kl)

    def gla_qk_task(which):
        b_a = ba_ref[...].reshape(nch, CHUNK, qk_a)
        b_mid = b_a[:, CHUNK // 2 - 1:CHUNK // 2, :]
        b_last = b_a[:, CHUNK - 1:CHUNK, :]
        if which == 0:
            bdec_ref[...] = jnp.exp(b_last)
            qa = mrow(proj(o_qa, qk_a) * (dk_a ** -0.5)).reshape(nch, CHUNK, qk_a)
            qe_ref[...] = (qa * jnp.exp(b_a - b_mid)).reshape(tc, qk_a).astype(BF16)
            qd_ref[...] = (qa * jnp.exp(b_a)).reshape(tc, qk_a).astype(BF16)
        else:
            ka = mrow(proj(o_ka, qk_a)).reshape(nch, CHUNK, qk_a)
            ke_ref[...] = (ka * jnp.exp(b_mid - b_a)).reshape(tc, qk_a).astype(BF16)
            kd_ref[...] = (ka * jnp.exp(b_last - b_a)).reshape(tc, qk_a).astype(BF16)

    side_tasks = [gla_gate_task,
                  functools.partial(gla_qk_task, 0), functools.partial(gla_qk_task, 1),
                  functools.partial(silu_task, sza_ref, o_za, 0), functools.partial(silu_task, sza_ref, o_za, 1),
                  functools.partial(silu_task, szb_ref, o_zb, 0), functools.partial(silu_task, szb_ref, o_zb, 1)]

    def side(n=1):
        for _ in range(n):
            if side_tasks:
                side_tasks.pop(0)()

    n_pairs = GDN_HEADS // 2
    for jp in range(n_pairs):
        if jp + 1 < n_pairs:
            gdn_proj(jp + 1)
        side(SIDE_PER_PAIR)
        gdn_prep(jp)

    tail = ubuf_ref[HIST + n_valid - (CONV_W - 1):HIST + n_valid, :]
    ubuf_ref[HIST - (CONV_W - 1):HIST, :] = tail
    conv_ref[0] = tail

    ri = lax.broadcasted_iota(jnp.int32, (CHUNK, CHUNK), 0)
    ci = lax.broadcasted_iota(jnp.int32, (CHUNK, CHUNK), 1)
    incl = ri >= ci
    strict = ri > ci
    eye = jnp.where(ri == ci, 1.0, 0.0).astype(F32)
    blk = {}
    s_ = INV_BASE
    while s_ <= CHUNK:
        sh = s_.bit_length() - 1
        blk[s_] = (ri >> sh) == (ci >> sh)
        s_ *= 2
    gna = gna_ref[...]
    gnb = gnb_ref[...]

    def chunk_rows(c):
        return pl.ds(pl.multiple_of(c * CHUNK, CHUNK), CHUNK)

    chains = [(c, h) for c in range(nch) for h in range(GDN_HEADS)]
    crow = [slice(c * CHUNK, (c + 1) * CHUNK) for c in range(nch)]
    hcols = lambda h: slice(h * GDN_DK, (h + 1) * GDN_DK)
    ks = [kn_ref[crow[c], hcols(h)] for c, h in chains]
    qs = [qn_ref[crow[c], hcols(h)] for c, h in chains]
    bcs = [bcol_ref[crow[c], :] for c in range(nch)]
    bes = [betac_ref[crow[c], :] for c in range(nch)]
    brs = [brow_ref[c] for c in range(nch)]
    kks = [_dot_nt(k, k) for k in ks]
    qks = [_dot_nt(q, k) for q, k in zip(qs, ks)]
    side()
    a_s, tinvs, pws = [], [], []
    for n_, (c, h) in enumerate(chains):
        diff = bcs[c][:, AIN_LANE + h:AIN_LANE + h + 1] - brs[c][h:h + 1, :]
        dm = jnp.where(incl, jnp.exp(jnp.where(incl, diff, 0.0)), 0.0)
        qkm_ref[c, h] = (qks[n_] * dm).astype(BF16)
        a = jnp.where(strict, bes[c][:, BETA_LANE + h:BETA_LANE + h + 1] * kks[n_] * dm, 0.0)
        dblk = jnp.where(blk[INV_BASE], a, 0.0)
        a_s.append(a)
        tinvs.append(eye - dblk)
        pws.append(dblk.astype(BF16))
    side()
    n = 2
    while n < INV_BASE:
        pws = [_dot(p, p).astype(BF16) for p in pws]
        tinvs = [t_ + _dot(t_.astype(BF16), p) for t_, p in zip(tinvs, pws)]
        side()
        n *= 2
    while n < CHUNK:
        es = [jnp.where(blk[2 * n] & ~blk[n], a, 0.0).astype(BF16) for a in a_s]
        tbs = [t_.astype(BF16) for t_ in tinvs]
        tes = [_dot(tb, e).astype(BF16) for tb, e in zip(tbs, es)]
        side()
        tinvs = [t_ - _dot(te, tb) for t_, te, tb in zip(tinvs, tes, tbs)]
        side()
        n *= 2
    for t_, (c, h) in zip(tinvs, chains):
        tb = t_.astype(BF16)
        uv_ref[crow[c], hcols(h)] = _dot(tb, bv_ref[crow[c], hcols(h)])
        wk_ref[crow[c], hcols(h)] = _dot(tb, bek_ref[crow[c], hcols(h)]).astype(BF16)
    side(len(side_tasks))

    def chunk_body(c, carry):
        rows = chunk_rows(c)
        ha, hb_ = range(GLA_HEADS), range(GDN_HEADS)
        lk = lambda h: slice(h * dk_a, (h + 1) * dk_a)
        lv = lambda h: slice(h * dv_a, (h + 1) * dv_a)
        hc = lambda h: slice(h * GDN_DK, (h + 1) * GDN_DK)
        sa = [sgla_ref[0, h] for h in ha]
        sb = [sgdn_ref[0, h] for h in hb_]
        v_a = [va_ref[rows, lv(h)] for h in ha]
        dec_a = bdec_ref[c]
        dv_all = dvec_ref[c]
        sab = [s_.astype(BF16) for s_ in sa]
        sbb = [s_.astype(BF16) for s_ in sb]
        ws = [_dot(wk_ref[rows, hc(h)], sbb[h]) for h in hb_]
        qsb = [_dot(qdec_ref[rows, hc(h)], sbb[h]) for h in hb_]
        att = [jnp.where(incl, _dot_nt(qe_ref[rows, lk(h)], ke_ref[rows, lk(h)]), 0.0).astype(BF16) for h in ha]
        oi = [_dot(qd_ref[rows, lk(h)], sab[h]) for h in ha]
        kv = [_dot_tn(kd_ref[rows, lk(h)], v_a[h]) for h in ha]
        u = [(uv_ref[rows, hc(h)] - ws[h]).astype(BF16) for h in hb_]
        o_b = [qsb[h] + _dot(qkm_ref[c, h], u[h]) for h in hb_]
        sb_new = [dv_all[h:h + 1, :] * sb[h] + _dot_tn(kdec_ref[rows, hc(h)], u[h]) for h in hb_]
        o_a = [_dot(att[h], v_a[h]) + oi[h] for h in ha]
        sa_new = []
        for h in ha:
            dcol = jnp.broadcast_to(dec_a[:, lk(h)], (dk_a, dk_a)).T
            sa_new.append(jnp.concatenate([dcol] * (dv_a // dk_a), axis=1) * sa[h] + kv[h])
        for h in ha:
            o = o_a[h]
            o = o * lax.rsqrt(jnp.mean(o * o, axis=-1, keepdims=True) + EPS) * gna
            oa_ref[rows, lv(h)] = (o * sza_ref[rows, lv(h)]).astype(BF16)
            sgla_ref[0, h] = sa_new[h]
        for h in hb_:
            o = o_b[h]
            o = o * lax.rsqrt(jnp.mean(o * o, axis=-1, keepdims=True) + EPS) * gnb
            ob_ref[rows, hc(h)] = (o * szb_ref[rows, hc(h)]).astype(BF16)
            sgdn_ref[0, h] = sb_new[h]
        return carry

    lax.fori_loop(0, nch, chunk_body, 0)

    merged = (jax.nn.sigmoid(proj(o_ga, d)) * _dot(oa_ref[...], wpa_ref[...])
              + jax.nn.sigmoid(proj(o_gb, d)) * _dot(ob_ref[...], wpb_ref[...]))
    out = _dot(merged.astype(BF16), wout_ref[...])
    xn = x_ref[0] + mod_ref[0][:, 2 * d:3 * d] * out
    if final_norm:
        xn = xn * lax.rsqrt(jnp.mean(xn * xn, axis=-1, keepdims=True) + EPS) * gfin_ref[...]
    y_ref[0] = xn


def _resident(shape):
    zeros = (0,) * len(shape)
    return pl.BlockSpec(shape, lambda b, t: zeros, pipeline_mode=pl.Buffered(1))


def _layer(x, mod, consts, states, *, tc, n_valid, final_norm):
    bsz, t_len, d = x.shape
    nch = tc // CHUNK
    qk_a = d // 2
    has_state = states is not None
    const_specs = [_resident(c.shape) for c in consts]
    in_specs = [pl.BlockSpec((1, tc, d), lambda b, t: (b, t, 0)),
                pl.BlockSpec((1, 1, 3 * d), lambda b, t: (b, 0, 0))] + const_specs
    args = [x, mod.reshape(bsz, 1, 3 * d)] + list(consts)
    state_shapes = [(bsz, GLA_HEADS, qk_a // GLA_HEADS, d // GLA_HEADS),
                    (bsz, GDN_HEADS, GDN_DK, GDN_DV),
                    (bsz, CONV_W - 1, 3 * d)]
    state_specs = [pl.BlockSpec((1,) + s[1:], lambda b, t, n=len(s): (b,) + (0,) * (n - 1)) for s in state_shapes]
    if has_state:
        in_specs += state_specs
        args += list(states)
    scratch = [
        pltpu.VMEM((tc, d), BF16),
        pltpu.VMEM((tc, qk_a), BF16), pltpu.VMEM((tc, qk_a), BF16),
        pltpu.VMEM((tc, qk_a), BF16), pltpu.VMEM((tc, qk_a), BF16),
        pltpu.VMEM((tc, d), BF16),
        pltpu.VMEM((tc, d), F32),
        pltpu.VMEM((nch, 1, qk_a), F32),
        pltpu.VMEM((HIST + tc, 3 * d), F32),
        pltpu.VMEM((tc, d), BF16), pltpu.VMEM((tc, d), BF16),
        pltpu.VMEM((tc, d), BF16), pltpu.VMEM((tc, d), BF16),
        pltpu.VMEM((tc, d), BF16), pltpu.VMEM((tc, d), BF16),
        pltpu.VMEM((tc, d), F32),
        pltpu.VMEM((tc, LANES), F32), pltpu.VMEM((tc, LANES), F32),
        pltpu.VMEM((nch, GDN_HEADS, CHUNK), F32),
        pltpu.VMEM((nch, GDN_HEADS, LANES), F32),
        pltpu.VMEM((tc, d), F32), pltpu.VMEM((tc, d), BF16),
        pltpu.VMEM((nch, GDN_HEADS, CHUNK, CHUNK), BF16),
        pltpu.VMEM((tc, qk_a), F32),
        pltpu.VMEM((tc, d), BF16), pltpu.VMEM((tc, d), BF16),
    ]
    kern = functools.partial(_layer_kernel, tc=tc, n_valid=n_valid, has_state=has_state,
                             final_norm=final_norm, d_model=d)
    return pl.pallas_call(
        kern,
        grid=(bsz, t_len // tc),
        in_specs=in_specs,
        out_specs=[pl.BlockSpec((1, tc, d), lambda b, t: (b, t, 0))] + state_specs,
        out_shape=[jax.ShapeDtypeStruct(x.shape, F32)] + [jax.ShapeDtypeStruct(s, F32) for s in state_shapes],
        scratch_shapes=scratch,
        compiler_params=pltpu.CompilerParams(dimension_semantics=("arbitrary", "arbitrary"),
                                             vmem_limit_bytes=VMEM_LIMIT_BYTES),
        name="gla_gdn_layer",
    )(*args)


def _layer_consts(tc, g_norm1, w_in, w_gk2, b_gk, w_conv, a_log, dt_bias, g_norm_a, g_norm_b,
                  w_pa, w_pb, w_out, g_final):
    d = w_in.shape[0]
    qk_a = d // 2
    o_gk = 2 * qk_a + 2 * d
    o_qkv = o_gk + GLA_RANK
    o_zb = o_qkv + 3 * d
    o_beta = o_zb + d
    o_a = o_beta + GDN_HEADS
    o_ga = o_a + GDN_HEADS
    pad = LANES - (GLA_RANK + 2 * GDN_HEADS)
    w_perm = jnp.concatenate(
        [w_in[:, 0:o_gk], w_in[:, o_qkv:o_beta], w_in[:, o_ga:o_ga + 2 * d],
         w_in[:, o_gk:o_qkv], w_in[:, o_beta:o_ga], jnp.zeros((d, pad), w_in.dtype)], axis=1).astype(BF16)
    w_at = w_in[:, o_a:o_ga].T.astype(BF16)
    w_gk = jnp.zeros((LANES, qk_a), F32).at[0:GLA_RANK].set(w_gk2).astype(BF16)
    lane_vec = lambda v: jnp.zeros((1, LANES), F32).at[0, AIN_LANE:AIN_LANE + GDN_HEADS].set(v)
    col_vec = lambda v: jnp.broadcast_to(v.reshape(GDN_HEADS, 1), (GDN_HEADS, CHUNK)).astype(F32)
    tt = jnp.arange(tc)
    bd = ((tt[:, None] // CHUNK == tt[None, :] // CHUNK) & (tt[None, :] <= tt[:, None])).astype(BF16)
    tu = jnp.arange(CHUNK)
    ut = (tu[:, None] <= tu[None, :]).astype(BF16)
    return [g_norm1.reshape(1, d), w_perm, w_at, w_gk, b_gk.reshape(1, qk_a), w_conv,
            lane_vec(a_log), lane_vec(dt_bias), col_vec(a_log), col_vec(dt_bias),
            g_norm_a.reshape(1, -1), g_norm_b.reshape(1, -1),
            w_pa.astype(BF16), w_pb.astype(BF16), w_out.astype(BF16), g_final.reshape(1, d), bd, ut]


PROMPT_TILE = 256


def kernel(x_prompt, x_sample, c_prompt, c_sample, state_gla, state_gdn, cache_conv_gdn, w_ada, b_ada, g_norm1, w_in, w_gk2, b_gk, w_conv, a_log, dt_bias, g_norm_a, g_norm_b, w_pa, w_pb, w_out, g_final):
    depth = w_in.shape[0]
    bp, tp, _ = x_prompt.shape
    bs, ts, _ = x_sample.shape
    assert tp % PROMPT_TILE == 0 and CONV_W - 1 <= ts <= CHUNK
    hp = x_prompt
    hs = jnp.pad(x_sample, ((0, 0), (0, CHUNK - ts), (0, 0)))
    outs_p, outs_s = [], []
    for layer in range(depth):
        last = layer == depth - 1
        mod = _adaln_mod(jnp.concatenate([c_prompt, c_sample], axis=0), w_ada[layer], b_ada[layer])
        lw = (g_norm1[layer], w_in[layer], w_gk2[layer], b_gk[layer], w_conv[layer], a_log[layer],
              dt_bias[layer], g_norm_a[layer], g_norm_b[layer], w_pa[layer], w_pb[layer], w_out[layer], g_final)
        hp, *st_p = _layer(hp, mod[:bp], _layer_consts(PROMPT_TILE, *lw), None,
                           tc=PROMPT_TILE, n_valid=PROMPT_TILE, final_norm=last)
        st_in = (state_gla[layer], state_gdn[layer], cache_conv_gdn[layer])
        hs, *st_s = _layer(hs, mod[bp:], _layer_consts(CHUNK, *lw), st_in,
                           tc=CHUNK, n_valid=ts, final_norm=last)
        outs_p.append(st_p)
        outs_s.append(st_s)

    stack = lambda outs, i: jnp.stack([o[i] for o in outs])
    return (hp, hs[:, :ts], stack(outs_p, 0), stack(outs_p, 1), stack(outs_p, 2),
            stack(outs_s, 0), stack(outs_s, 1), stack(outs_s, 2))
```

```python
import functools

import jax
import jax.numpy as jnp
from jax import lax
from jax.experimental import pallas as pl
from jax.experimental.pallas import tpu as pltpu

F32 = jnp.float32
BF16 = jnp.bfloat16

CHUNK = 64
EPS = 1e-6
GLA_HEADS = 4
GLA_RANK = 16
GLA_GATE_NORM = 16.0
GDN_HEADS = 8
GDN_DK = 128
GDN_DV = 128
CONV_W = 4
LANES = 128
HIST = 8
INV_BASE = 8
SIDE_PER_PAIR = 2
BETA_LANE = GLA_RANK
AIN_LANE = GLA_RANK + GDN_HEADS
VMEM_LIMIT_BYTES = 56 * 1024 * 1024


def _dot(a, b):
    return jnp.dot(a, b, preferred_element_type=F32)


def _dot_nt(a, b):
    return lax.dot_general(a, b, (((1,), (1,)), ((), ())), preferred_element_type=F32)


def _dot_tn(a, b):
    return lax.dot_general(a, b, (((0,), (0,)), ((), ())), preferred_element_type=F32)


def _split2(x):
    hi = x.astype(BF16)
    return hi, (x - hi.astype(F32)).astype(BF16)


def _split3(x):
    hi = x.astype(BF16)
    r = x - hi.astype(F32)
    mid = r.astype(BF16)
    lo = (r - mid.astype(F32)).astype(BF16)
    return hi, mid, lo


def _softplus(x):
    return jnp.maximum(x, 0.0) + jnp.log1p(jnp.exp(-jnp.abs(x)))


def _log_sigmoid(x):
    return jnp.minimum(x, 0.0) - jnp.log1p(jnp.exp(-jnp.abs(x)))


def _silu(x):
    return x * jax.nn.sigmoid(x)


def _mod_kernel(c_ref, w_ref, b_ref, o_ref):
    s = _silu(c_ref[...]).astype(BF16)
    o_ref[...] = _dot(s, w_ref[...].astype(BF16)) + b_ref[...]


def _adaln_mod(c, w_ada, b_ada):
    n, d = c.shape
    d3 = w_ada.shape[1]
    bn = 512
    return pl.pallas_call(
        _mod_kernel,
        grid=(d3 // bn,),
        in_specs=[pl.BlockSpec((n, d), lambda j: (0, 0)),
                  pl.BlockSpec((d, bn), lambda j: (0, j)),
                  pl.BlockSpec((1, bn), lambda j: (0, j))],
        out_specs=pl.BlockSpec((n, bn), lambda j: (0, j)),
        out_shape=jax.ShapeDtypeStruct((n, d3), F32),
        name="adaln_mod",
    )(c, w_ada, b_ada.reshape(1, d3))


def _layer_kernel(*refs, tc, n_valid, has_state, final_norm, d_model):
    d = d_model
    qk_a = d // 2
    dk_a = qk_a // GLA_HEADS
    dv_a = d // GLA_HEADS
    nch = tc // CHUNK
    o_qa, o_ka, o_va, o_za = 0, qk_a, 2 * qk_a, 2 * qk_a + d
    o_qb = o_za + d
    o_kb, o_vb = o_qb + d, o_qb + 2 * d
    o_zb = o_qb + 3 * d
    o_ga, o_gb = o_zb + d, o_zb + 2 * d
    o_sm = o_gb + d

    it = iter(refs)
    x_ref, mod_ref, g1_ref, win_ref, wat_ref, wgk_ref, bgk_ref, wconv_ref = (next(it) for _ in range(8))
    alane_ref, dlane_ref, acol_ref, dcol_ref = (next(it) for _ in range(4))
    gna_ref, gnb_ref, wpa_ref, wpb_ref, wout_ref, gfin_ref, bd_ref, ut_ref = (next(it) for _ in range(8))
    if has_state:
        sgla_in, sgdn_in, conv_in = (next(it) for _ in range(3))
    y_ref, sgla_ref, sgdn_ref, conv_ref = (next(it) for _ in range(4))
    (hb_ref, qe_ref, ke_ref, qd_ref, kd_ref, va_ref, sza_ref, bdec_ref, ubuf_ref,
     qn_ref, kn_ref, qdec_ref, kdec_ref, bv_ref, bek_ref, szb_ref,
     bcol_ref, betac_ref, brow_ref, dvec_ref, uv_ref, wk_ref, qkm_ref, ba_ref, oa_ref, ob_ref) = (next(it) for _ in range(26))

    t = pl.program_id(1)

    @pl.when(t == 0)
    def _init():
        if has_state:
            sgla_ref[...] = sgla_in[...]
            sgdn_ref[...] = sgdn_in[...]
            ubuf_ref[HIST - (CONV_W - 1):HIST, :] = conv_in[0]
        else:
            sgla_ref[...] = jnp.zeros_like(sgla_ref)
            sgdn_ref[...] = jnp.zeros_like(sgdn_ref)
            ubuf_ref[0:HIST, :] = jnp.zeros((HIST, ubuf_ref.shape[1]), F32)

    masked = n_valid < tc
    if masked:
        rowmask = lax.broadcasted_iota(jnp.int32, (tc, 1), 0) < n_valid

    def mrow(v):
        return jnp.where(rowmask, v, 0.0) if masked else v

    x = x_ref[0]
    mod = mod_ref[0]
    shift, scale = mod[:, 0:d], mod[:, d:2 * d]
    hn = x * lax.rsqrt(jnp.mean(x * x, axis=-1, keepdims=True) + EPS) * g1_ref[...]
    hb_ref[...] = (hn * (1.0 + scale) + shift).astype(BF16)

    def proj(c0, width):
        return _dot(hb_ref[...], win_ref[:, c0:c0 + width])

    pair_w = 2 * GDN_DK
    half = d // 2

    def gdn_proj(jp, parts=(0, 1, 2)):
        for j in parts:
            c0 = j * d + jp * pair_w
            ubuf_ref[HIST:HIST + tc, c0:c0 + pair_w] = proj(o_qb + c0, pair_w)

    def va_task(i):
        va_ref[:, i * half:(i + 1) * half] = mrow(proj(o_va + i * half, half)).astype(BF16)

    def silu_task(dst_ref, c0, i):
        dst_ref[:, i * half:(i + 1) * half] = _silu(proj(c0 + i * half, half))

    ps = proj(o_sm, LANES)
    arows = [_dot_nt(wat_ref[...], hb_ref[c * CHUNK:(c + 1) * CHUNK, :]) for c in range(nch)]
    gdn_proj(0, (0,))
    lane = lax.broadcasted_iota(jnp.int32, (1, LANES), 1)
    betac_ref[...] = mrow(jax.nn.sigmoid(ps))
    g_col = -jnp.exp(alane_ref[...]) * _softplus(ps + dlane_ref[...])
    g_col = mrow(jnp.where((lane >= AIN_LANE) & (lane < AIN_LANE + GDN_HEADS), g_col, 0.0))
    g_rows = []
    for c in range(nch):
        g_row = -jnp.exp(acol_ref[...]) * _softplus(arows[c] + dcol_ref[...])
        if masked:
            colmask = (lax.broadcasted_iota(jnp.int32, (1, CHUNK), 1) + c * CHUNK) < n_valid
            g_row = jnp.where(colmask, g_row, 0.0)
        g_rows.append(g_row)
    gdn_proj(0, (1,))
    bd = bd_ref[...]
    ut = ut_ref[...]
    gh, gm, gl = _split3(g_col)
    b_col = _dot(bd, gh) + _dot(bd, gm) + _dot(bd, gl)
    b_rows = []
    for c in range(nch):
        rh, rm, rl = _split3(g_rows[c])
        b_rows.append(_dot(rh, ut) + _dot(rm, ut) + _dot(rl, ut))
    gdn_proj(0, (2,))
    va_task(0)
    bcol_ref[...] = b_col
    for c in range(nch):
        brow_ref[c] = b_rows[c]
        dvec_ref[c] = jnp.broadcast_to(jnp.exp(b_rows[c][:, CHUNK - 1:CHUNK]), (GDN_HEADS, LANES))
    e_b = jnp.exp(b_col)
    b_col3 = b_col.reshape(nch, CHUNK, LANES)
    e_lb = jnp.exp(b_col3[:, CHUNK - 1:CHUNK, :] - b_col3).reshape(tc, LANES)
    beta = betac_ref[...]
    va_task(1)

    def conv(c0, width):
        cols = slice(c0, c0 + width)
        full = ubuf_ref[:, cols]
        acc = full[HIST:] * wconv_ref[CONV_W - 1:CONV_W, cols]
        for i in range(CONV_W - 1):
            back = CONV_W - 1 - i
            acc = acc + pltpu.roll(full, back, 0)[HIST:] * wconv_ref[i:i + 1, cols]
        return _silu(acc)

    def gdn_prep(jp):
        cq, ck, cv = (conv(j * d + jp * pair_w, pair_w) for j in range(3))
        for hh in range(2):
            h = 2 * jp + hh
            cols = slice(h * GDN_DK, (h + 1) * GDN_DK)
            loc = slice(hh * GDN_DK, (hh + 1) * GDN_DK)
            be_h = beta[:, BETA_LANE + h:BETA_LANE + h + 1]
            eb_h = e_b[:, AIN_LANE + h:AIN_LANE + h + 1]
            elb_h = e_lb[:, AIN_LANE + h:AIN_LANE + h + 1]
            qh = cq[:, loc]
            qh = mrow(qh * lax.rsqrt(jnp.sum(qh * qh, axis=-1, keepdims=True) + EPS) * (GDN_DK ** -0.5))
            kh_ = ck[:, loc]
            kh_ = mrow(kh_ * lax.rsqrt(jnp.sum(kh_ * kh_, axis=-1, keepdims=True) + EPS))
            vh = mrow(cv[:, loc])
            qn_ref[:, cols] = qh.astype(BF16)
            kn_ref[:, cols] = kh_.astype(BF16)
            qdec_ref[:, cols] = (qh * eb_h).astype(BF16)
            kdec_ref[:, cols] = (kh_ * elb_h).astype(BF16)
            bv_ref[:, cols] = (be_h * vh).astype(BF16)
            bek_ref[:, cols] = ((be_h * eb_h) * kh_).astype(BF16)

    def gla_gate_task():
        gk = _log_sigmoid(_dot(ps.astype(BF16), wgk_ref[...]) + bgk_ref[...]) * (1.0 / GLA_GATE_NORM)
        gk = mrow(gk)
        kh, kl = _split2(gk)
        ba_ref[...] = _dot(bd, kh) + _dot(bd, kl)

    def gla_qk_task(which):
        b_a = ba_ref[...].reshape(nch, CHUNK, qk_a)
        b_mid = b_a[:, CHUNK // 2 - 1:CHUNK // 2, :]
        b_last = b_a[:, CHUNK - 1:CHUNK, :]
        if which == 0:
            bdec_ref[...] = jnp.exp(b_last)
            qa = mrow(proj(o_qa, qk_a) * (dk_a ** -0.5)).reshape(nch, CHUNK, qk_a)
            qe_ref[...] = (qa * jnp.exp(b_a - b_mid)).reshape(tc, qk_a).astype(BF16)
            qd_ref[...] = (qa * jnp.exp(b_a)).reshape(tc, qk_a).astype(BF16)
        else:
            ka = mrow(proj(o_ka, qk_a)).reshape(nch, CHUNK, qk_a)
            ke_ref[...] = (ka * jnp.exp(b_mid - b_a)).reshape(tc, qk_a).astype(BF16)
            kd_ref[...] = (ka * jnp.exp(b_last - b_a)).reshape(tc, qk_a).astype(BF16)

    side_tasks = [gla_gate_task,
                  functools.partial(gla_qk_task, 0), functools.partial(gla_qk_task, 1),
                  functools.partial(silu_task, sza_ref, o_za, 0), functools.partial(silu_task, sza_ref, o_za, 1),
                  functools.partial(silu_task, szb_ref, o_zb, 0), functools.partial(silu_task, szb_ref, o_zb, 1)]

    def side(n=1):
        for _ in range(n):
            if side_tasks:
                side_tasks.pop(0)()

    n_pairs = GDN_HEADS // 2
    for jp in range(n_pairs):
        if jp + 1 < n_pairs:
            gdn_proj(jp + 1)
        side(SIDE_PER_PAIR)
        gdn_prep(jp)

    tail = ubuf_ref[HIST + n_valid - (CONV_W - 1):HIST + n_valid, :]
    ubuf_ref[HIST - (CONV_W - 1):HIST, :] = tail
    conv_ref[0] = tail

    npair = GDN_HEADS // 2
    ri = lax.broadcasted_iota(jnp.int32, (CHUNK, 2 * CHUNK), 0)
    li = lax.broadcasted_iota(jnp.int32, (CHUNK, 2 * CHUNK), 1)
    ci = li & (CHUNK - 1)
    lo = li < CHUNK
    incl = ri >= ci
    strict = ri > ci
    eye = jnp.where(ri == ci, 1.0, 0.0).astype(F32)
    blk = {}
    s_ = INV_BASE
    while s_ <= CHUNK:
        sh = s_.bit_length() - 1
        blk[s_] = (ri >> sh) == (ci >> sh)
        s_ *= 2
    incl1 = (lax.broadcasted_iota(jnp.int32, (CHUNK, CHUNK), 0)
             >= lax.broadcasted_iota(jnp.int32, (CHUNK, CHUNK), 1))
    gna = gna_ref[...]
    gnb = gnb_ref[...]
    zblk = jnp.zeros((CHUNK, LANES), BF16)

    def bdiag_packed(y):
        return jnp.concatenate([jnp.where(lo, y, 0), jnp.where(lo, 0, y)], axis=0)

    def bdiag_wide(y):
        return jnp.concatenate([jnp.concatenate([y[:, 0:LANES], zblk], axis=1),
                                jnp.concatenate([zblk, y[:, LANES:2 * LANES]], axis=1)], axis=0)

    def chunk_rows(c):
        return pl.ds(pl.multiple_of(c * CHUNK, CHUNK), CHUNK)

    chains = [(c, p) for c in range(nch) for p in range(npair)]
    crow = [slice(c * CHUNK, (c + 1) * CHUNK) for c in range(nch)]
    pcols = lambda p: slice(p * pair_w, (p + 1) * pair_w)
    kbds = [bdiag_wide(kn_ref[crow[c], pcols(p)]) for c, p in chains]
    kks = [_dot_nt(kn_ref[crow[c], pcols(p)], kbd) for (c, p), kbd in zip(chains, kbds)]
    qks = [_dot_nt(qn_ref[crow[c], pcols(p)], kbd) for (c, p), kbd in zip(chains, kbds)]
    bcs = [bcol_ref[crow[c], :] for c in range(nch)]
    bes = [betac_ref[crow[c], :] for c in range(nch)]
    brs = [brow_ref[c] for c in range(nch)]
    side()
    a_s, tinvs, pws = [], [], []
    for n_, (c, p) in enumerate(chains):
        h1, h2 = 2 * p, 2 * p + 1
        pick = lambda v, l0: jnp.where(lo, v[:, l0 + h1:l0 + h1 + 1], v[:, l0 + h2:l0 + h2 + 1])
        diff = pick(bcs[c], AIN_LANE) - jnp.where(lo[0:1], brs[c][h1:h1 + 1, :], brs[c][h2:h2 + 1, :])
        dm = jnp.where(incl, jnp.exp(jnp.where(incl, diff, 0.0)), 0.0)
        qkm_ref[c, p] = (qks[n_] * dm).astype(BF16)
        a = jnp.where(strict, pick(bes[c], BETA_LANE) * kks[n_] * dm, 0.0)
        dblk = jnp.where(blk[INV_BASE], a, 0.0)
        a_s.append(a)
        tinvs.append(eye - dblk)
        pws.append(dblk.astype(BF16))
    side()
    n = 2
    while n < INV_BASE:
        pws = [_dot(p_, bdiag_packed(p_)).astype(BF16) for p_ in pws]
        tinvs = [t_ + _dot(t_.astype(BF16), bdiag_packed(p_)) for t_, p_ in zip(tinvs, pws)]
        side()
        n *= 2
    while n < CHUNK:
        es = [jnp.where(blk[2 * n] & ~blk[n], a, 0.0).astype(BF16) for a in a_s]
        tbs = [t_.astype(BF16) for t_ in tinvs]
        tbds = [bdiag_packed(tb) for tb in tbs]
        tes = [_dot(tb, bdiag_packed(e)).astype(BF16) for tb, e in zip(tbs, es)]
        side()
        tinvs = [t_ - _dot(te, tbd) for t_, te, tbd in zip(tinvs, tes, tbds)]
        side()
        n *= 2
    for t_, (c, p) in zip(tinvs, chains):
        tb = t_.astype(BF16)
        uv_ref[crow[c], pcols(p)] = _dot(tb, bdiag_wide(bv_ref[crow[c], pcols(p)]))
        wk_ref[crow[c], pcols(p)] = _dot(tb, bdiag_wide(bek_ref[crow[c], pcols(p)])).astype(BF16)
    side(len(side_tasks))

    def chunk_body(c, carry):
        rows = chunk_rows(c)
        ha, hb_ = range(GLA_HEADS), range(GDN_HEADS)
        lk = lambda h: slice(h * dk_a, (h + 1) * dk_a)
        lv = lambda h: slice(h * dv_a, (h + 1) * dv_a)
        hc = lambda h: slice(h * GDN_DK, (h + 1) * GDN_DK)
        sa = [sgla_ref[0, h] for h in ha]
        sb = [sgdn_ref[0, h] for h in hb_]
        v_a = [va_ref[rows, lv(h)] for h in ha]
        dec_a = bdec_ref[c]
        dv_all = dvec_ref[c]
        sab = [s_.astype(BF16) for s_ in sa]
        sbb = [s_.astype(BF16) for s_ in sb]
        wq = [_dot(jnp.concatenate([wk_ref[rows, hc(h)], qdec_ref[rows, hc(h)]], axis=0), sbb[h]) for h in hb_]
        ws = [w_[0:CHUNK] for w_ in wq]
        qsb = [w_[CHUNK:2 * CHUNK] for w_ in wq]
        att = [jnp.where(incl1, _dot_nt(qe_ref[rows, lk(h)], ke_ref[rows, lk(h)]), 0.0).astype(BF16) for h in ha]
        oi = [_dot(qd_ref[rows, lk(h)], sab[h]) for h in ha]
        kv = [_dot_tn(kd_ref[rows, lk(h)], v_a[h]) for h in ha]
        u = [(uv_ref[rows, hc(h)] - ws[h]).astype(BF16) for h in hb_]
        qku = [_dot(qkm_ref[c, p], bdiag_wide(jnp.concatenate([u[2 * p], u[2 * p + 1]], axis=1)))
               for p in range(npair)]
        o_b = [qsb[h] + qku[h // 2][:, (h % 2) * GDN_DV:(h % 2 + 1) * GDN_DV] for h in hb_]
        sb_new = [dv_all[h:h + 1, :] * sb[h] + _dot_tn(kdec_ref[rows, hc(h)], u[h]) for h in hb_]
        o_a = [_dot(att[h], v_a[h]) + oi[h] for h in ha]
        sa_new = []
        for h in ha:
            dcol = jnp.broadcast_to(dec_a[:, lk(h)], (dk_a, dk_a)).T
            sa_new.append(jnp.concatenate([dcol] * (dv_a // dk_a), axis=1) * sa[h] + kv[h])
        for h in ha:
            o = o_a[h]
            o = o * lax.rsqrt(jnp.mean(o * o, axis=-1, keepdims=True) + EPS) * gna
            oa_ref[rows, lv(h)] = (o * sza_ref[rows, lv(h)]).astype(BF16)
            sgla_ref[0, h] = sa_new[h]
        for h in hb_:
            o = o_b[h]
            o = o * lax.rsqrt(jnp.mean(o * o, axis=-1, keepdims=True) + EPS) * gnb
            ob_ref[rows, hc(h)] = (o * szb_ref[rows, hc(h)]).astype(BF16)
            sgdn_ref[0, h] = sb_new[h]
        return carry

    lax.fori_loop(0, nch, chunk_body, 0)

    merged = (jax.nn.sigmoid(proj(o_ga, d)) * _dot(oa_ref[...], wpa_ref[...])
              + jax.nn.sigmoid(proj(o_gb, d)) * _dot(ob_ref[...], wpb_ref[...]))
    out = _dot(merged.astype(BF16), wout_ref[...])
    xn = x_ref[0] + mod_ref[0][:, 2 * d:3 * d] * out
    if final_norm:
        xn = xn * lax.rsqrt(jnp.mean(xn * xn, axis=-1, keepdims=True) + EPS) * gfin_ref[...]
    y_ref[0] = xn


def _resident(shape):
    zeros = (0,) * len(shape)
    return pl.BlockSpec(shape, lambda b, t: zeros, pipeline_mode=pl.Buffered(1))


def _layer(x, mod, consts, states, *, tc, n_valid, final_norm):
    bsz, t_len, d = x.shape
    nch = tc // CHUNK
    qk_a = d // 2
    has_state = states is not None
    const_specs = [_resident(c.shape) for c in consts]
    in_specs = [pl.BlockSpec((1, tc, d), lambda b, t: (b, t, 0)),
                pl.BlockSpec((1, 1, 3 * d), lambda b, t: (b, 0, 0))] + const_specs
    args = [x, mod.reshape(bsz, 1, 3 * d)] + list(consts)
    state_shapes = [(bsz, GLA_HEADS, qk_a // GLA_HEADS, d // GLA_HEADS),
                    (bsz, GDN_HEADS, GDN_DK, GDN_DV),
                    (bsz, CONV_W - 1, 3 * d)]
    state_specs = [pl.BlockSpec((1,) + s[1:], lambda b, t, n=len(s): (b,) + (0,) * (n - 1)) for s in state_shapes]
    if has_state:
        in_specs += state_specs
        args += list(states)
    scratch = [
        pltpu.VMEM((tc, d), BF16),
        pltpu.VMEM((tc, qk_a), BF16), pltpu.VMEM((tc, qk_a), BF16),
        pltpu.VMEM((tc, qk_a), BF16), pltpu.VMEM((tc, qk_a), BF16),
        pltpu.VMEM((tc, d), BF16),
        pltpu.VMEM((tc, d), F32),
        pltpu.VMEM((nch, 1, qk_a), F32),
        pltpu.VMEM((HIST + tc, 3 * d), F32),
        pltpu.VMEM((tc, d), BF16), pltpu.VMEM((tc, d), BF16),
        pltpu.VMEM((tc, d), BF16), pltpu.VMEM((tc, d), BF16),
        pltpu.VMEM((tc, d), BF16), pltpu.VMEM((tc, d), BF16),
        pltpu.VMEM((tc, d), F32),
        pltpu.VMEM((tc, LANES), F32), pltpu.VMEM((tc, LANES), F32),
        pltpu.VMEM((nch, GDN_HEADS, 2 * CHUNK), F32),
        pltpu.VMEM((nch, GDN_HEADS, LANES), F32),
        pltpu.VMEM((tc, d), F32), pltpu.VMEM((tc, d), BF16),
        pltpu.VMEM((nch, GDN_HEADS // 2, CHUNK, 2 * CHUNK), BF16),
        pltpu.VMEM((tc, qk_a), F32),
        pltpu.VMEM((tc, d), BF16), pltpu.VMEM((tc, d), BF16),
    ]
    kern = functools.partial(_layer_kernel, tc=tc, n_valid=n_valid, has_state=has_state,
                             final_norm=final_norm, d_model=d)
    return pl.pallas_call(
        kern,
        grid=(bsz, t_len // tc),
        in_specs=in_specs,
        out_specs=[pl.BlockSpec((1, tc, d), lambda b, t: (b, t, 0))] + state_specs,
        out_shape=[jax.ShapeDtypeStruct(x.shape, F32)] + [jax.ShapeDtypeStruct(s, F32) for s in state_shapes],
        scratch_shapes=scratch,
        compiler_params=pltpu.CompilerParams(dimension_semantics=("arbitrary", "arbitrary"),
                                             vmem_limit_bytes=VMEM_LIMIT_BYTES),
        name="gla_gdn_layer",
    )(*args)


def _layer_consts(tc, g_norm1, w_in, w_gk2, b_gk, w_conv, a_log, dt_bias, g_norm_a, g_norm_b,
                  w_pa, w_pb, w_out, g_final):
    d = w_in.shape[0]
    qk_a = d // 2
    o_gk = 2 * qk_a + 2 * d
    o_qkv = o_gk + GLA_RANK
    o_zb = o_qkv + 3 * d
    o_beta = o_zb + d
    o_a = o_beta + GDN_HEADS
    o_ga = o_a + GDN_HEADS
    pad = LANES - (GLA_RANK + 2 * GDN_HEADS)
    w16 = w_in.astype(BF16)
    w_perm = jnp.concatenate(
        [w16[:, 0:o_gk], w16[:, o_qkv:o_beta], w16[:, o_ga:o_ga + 2 * d],
         w16[:, o_gk:o_qkv], w16[:, o_beta:o_ga], jnp.zeros((d, pad), BF16)], axis=1)
    w_at = w16[:, o_a:o_ga].T
    w_gk = jnp.zeros((LANES, qk_a), F32).at[0:GLA_RANK].set(w_gk2).astype(BF16)
    lane_vec = lambda v: jnp.zeros((1, LANES), F32).at[0, AIN_LANE:AIN_LANE + GDN_HEADS].set(v)
    col_vec = lambda v: jnp.broadcast_to(v.reshape(GDN_HEADS, 1), (GDN_HEADS, CHUNK)).astype(F32)
    tt = jnp.arange(tc)
    bd = ((tt[:, None] // CHUNK == tt[None, :] // CHUNK) & (tt[None, :] <= tt[:, None])).astype(BF16)
    tu = jnp.arange(CHUNK)
    ut = jnp.tile(tu[:, None] <= tu[None, :], (1, 2)).astype(BF16)
    return [g_norm1.reshape(1, d), w_perm, w_at, w_gk, b_gk.reshape(1, qk_a), w_conv,
            lane_vec(a_log), lane_vec(dt_bias), col_vec(a_log), col_vec(dt_bias),
            g_norm_a.reshape(1, -1), g_norm_b.reshape(1, -1),
            w_pa.astype(BF16), w_pb.astype(BF16), w_out.astype(BF16), g_final.reshape(1, d), bd, ut]


PROMPT_TILE = 256


def kernel(x_prompt, x_sample, c_prompt, c_sample, state_gla, state_gdn, cache_conv_gdn, w_ada, b_ada, g_norm1, w_in, w_gk2, b_gk, w_conv, a_log, dt_bias, g_norm_a, g_norm_b, w_pa, w_pb, w_out, g_final):
    depth = w_in.shape[0]
    bp, tp, _ = x_prompt.shape
    bs, ts, _ = x_sample.shape
    assert tp % PROMPT_TILE == 0 and CONV_W - 1 <= ts <= CHUNK
    hp = x_prompt
    hs = jnp.pad(x_sample, ((0, 0), (0, CHUNK - ts), (0, 0)))
    outs_p, outs_s = [], []
    for layer in range(depth):
        last = layer == depth - 1
        mod = _adaln_mod(jnp.concatenate([c_prompt, c_sample], axis=0), w_ada[layer], b_ada[layer])
        lw = (g_norm1[layer], w_in[layer], w_gk2[layer], b_gk[layer], w_conv[layer], a_log[layer],
              dt_bias[layer], g_norm_a[layer], g_norm_b[layer], w_pa[layer], w_pb[layer], w_out[layer], g_final)
        hp, *st_p = _layer(hp, mod[:bp], _layer_consts(PROMPT_TILE, *lw), None,
                           tc=PROMPT_TILE, n_valid=PROMPT_TILE, final_norm=last)
        st_in = (state_gla[layer], state_gdn[layer], cache_conv_gdn[layer])
        hs, *st_s = _layer(hs, mod[bp:], _layer_consts(CHUNK, *lw), st_in,
                           tc=CHUNK, n_valid=ts, final_norm=last)
        outs_p.append(st_p)
        outs_s.append(st_s)

    stack = lambda outs, i: jnp.stack([o[i] for o in outs])
    return (hp, hs[:, :ts], stack(outs_p, 0), stack(outs_p, 1), stack(outs_p, 2),
            stack(outs_s, 0), stack(outs_s, 1), stack(outs_s, 2))
```

```python
import functools

import jax
import jax.numpy as jnp
from jax import lax
from jax.experimental import pallas as pl
from jax.experimental.pallas import tpu as pltpu

F32 = jnp.float32
BF16 = jnp.bfloat16

CHUNK = 64
EPS = 1e-6
GLA_HEADS = 4
GLA_RANK = 16
GLA_GATE_NORM = 16.0
GDN_HEADS = 8
GDN_DK = 128
GDN_DV = 128
CONV_W = 4
LANES = 128
HIST = 8
INV_BASE = 8
SIDE_PER_PAIR = 2
BETA_LANE = GLA_RANK
AIN_LANE = GLA_RANK + GDN_HEADS
VMEM_LIMIT_BYTES = 56 * 1024 * 1024


def _dot(a, b):
    return jnp.dot(a, b, preferred_element_type=F32)


def _dot_nt(a, b):
    return lax.dot_general(a, b, (((1,), (1,)), ((), ())), preferred_element_type=F32)


def _dot_tn(a, b):
    return lax.dot_general(a, b, (((0,), (0,)), ((), ())), preferred_element_type=F32)


def _split2(x):
    hi = x.astype(BF16)
    return hi, (x - hi.astype(F32)).astype(BF16)


def _split3(x):
    hi = x.astype(BF16)
    r = x - hi.astype(F32)
    mid = r.astype(BF16)
    lo = (r - mid.astype(F32)).astype(BF16)
    return hi, mid, lo


def _softplus(x):
    return jnp.maximum(x, 0.0) + jnp.log1p(jnp.exp(-jnp.abs(x)))


def _log_sigmoid(x):
    return jnp.minimum(x, 0.0) - jnp.log1p(jnp.exp(-jnp.abs(x)))


def _silu(x):
    return x * jax.nn.sigmoid(x)


def _mod_kernel(c_ref, w_ref, b_ref, o_ref):
    s = _silu(c_ref[...]).astype(BF16)
    o_ref[...] = _dot(s, w_ref[...].astype(BF16)) + b_ref[...]


def _adaln_mod(c, w_ada, b_ada):
    n, d = c.shape
    d3 = w_ada.shape[1]
    bn = 512
    return pl.pallas_call(
        _mod_kernel,
        grid=(d3 // bn,),
        in_specs=[pl.BlockSpec((n, d), lambda j: (0, 0)),
                  pl.BlockSpec((d, bn), lambda j: (0, j)),
                  pl.BlockSpec((1, bn), lambda j: (0, j))],
        out_specs=pl.BlockSpec((n, bn), lambda j: (0, j)),
        out_shape=jax.ShapeDtypeStruct((n, d3), F32),
        name="adaln_mod",
    )(c, w_ada, b_ada.reshape(1, d3))


def _layer_kernel(*refs, tc, n_valid, has_state, final_norm, d_model):
    d = d_model
    qk_a = d // 2
    dk_a = qk_a // GLA_HEADS
    dv_a = d // GLA_HEADS
    nch = tc // CHUNK
    o_qa, o_ka, o_va, o_za = 0, qk_a, 2 * qk_a, 2 * qk_a + d
    o_qb = o_za + d
    o_kb, o_vb = o_qb + d, o_qb + 2 * d
    o_zb = o_qb + 3 * d
    o_ga, o_gb = o_zb + d, o_zb + 2 * d
    o_sm = o_gb + d

    it = iter(refs)
    x_ref, mod_ref, g1_ref, wa_ref, wb_ref, wc_ref, wsm_ref, wgk_ref, bgk_ref, wconv_ref = (next(it) for _ in range(10))
    alane_ref, dlane_ref = (next(it) for _ in range(2))
    gna_ref, gnb_ref, wpa_ref, wpb_ref, wout_ref, gfin_ref, bd_ref = (next(it) for _ in range(7))
    if has_state:
        sgla_in, sgdn_in, conv_in = (next(it) for _ in range(3))
    y_ref, sgla_ref, sgdn_ref, conv_ref = (next(it) for _ in range(4))
    (hb_ref, qe_ref, ke_ref, qd_ref, kd_ref, va_ref, sza_ref, bdec_ref, ubuf_ref,
     qn_ref, kn_ref, qdec_ref, kdec_ref, bv_ref, bek_ref, szb_ref,
     bcol_ref, betac_ref, brow_ref, dvec_ref, uv_ref, wk_ref, qkm_ref, ba_ref, oa_ref, ob_ref) = (next(it) for _ in range(26))

    t = pl.program_id(1)

    @pl.when(t == 0)
    def _init():
        if has_state:
            sgla_ref[...] = sgla_in[...]
            sgdn_ref[...] = sgdn_in[...]
            ubuf_ref[HIST - (CONV_W - 1):HIST, :] = conv_in[0]
        else:
            sgla_ref[...] = jnp.zeros_like(sgla_ref)
            sgdn_ref[...] = jnp.zeros_like(sgdn_ref)
            ubuf_ref[0:HIST, :] = jnp.zeros((HIST, ubuf_ref.shape[1]), F32)

    masked = n_valid < tc
    if masked:
        rowmask = lax.broadcasted_iota(jnp.int32, (tc, 1), 0) < n_valid

    def mrow(v):
        return jnp.where(rowmask, v, 0.0) if masked else v

    x = x_ref[0]
    mod = mod_ref[0]
    shift, scale = mod[:, 0:d], mod[:, d:2 * d]
    hn = x * lax.rsqrt(jnp.mean(x * x, axis=-1, keepdims=True) + EPS) * g1_ref[...]
    hb_ref[...] = (hn * (1.0 + scale) + shift).astype(BF16)

    w_segments = ((o_qa, wa_ref), (o_qb, wb_ref), (o_ga, wc_ref), (o_sm, wsm_ref))

    def proj(c0, width):
        start, w_ref = [seg for seg in w_segments if seg[0] <= c0][-1]
        return _dot(hb_ref[...], w_ref[:, c0 - start:c0 - start + width])

    pair_w = 2 * GDN_DK
    half = d // 2

    def gdn_proj(jp, parts=(0, 1, 2)):
        for j in parts:
            c0 = j * d + jp * pair_w
            ubuf_ref[HIST:HIST + tc, c0:c0 + pair_w] = proj(o_qb + c0, pair_w)

    def va_task(i):
        va_ref[:, i * half:(i + 1) * half] = mrow(proj(o_va + i * half, half)).astype(BF16)

    def silu_task(dst_ref, c0, i):
        dst_ref[:, i * half:(i + 1) * half] = _silu(proj(c0 + i * half, half))

    ps = proj(o_sm, LANES)
    gdn_proj(0, (0,))
    lane = lax.broadcasted_iota(jnp.int32, (1, LANES), 1)
    betac_ref[...] = mrow(jax.nn.sigmoid(ps))
    g_col = -jnp.exp(alane_ref[...]) * _softplus(ps + dlane_ref[...])
    g_col = mrow(jnp.where((lane >= AIN_LANE) & (lane < AIN_LANE + GDN_HEADS), g_col, 0.0))
    gdn_proj(0, (1,))
    bd = bd_ref[...]
    gh, gm, gl = _split3(g_col)
    b_col = _dot(bd, gh) + _dot(bd, gm) + _dot(bd, gl)
    gdn_proj(0, (2,))
    va_task(0)
    bcol_ref[...] = b_col
    e_b = jnp.exp(b_col)
    b_col3 = b_col.reshape(nch, CHUNK, LANES)
    e_lb = jnp.exp(b_col3[:, CHUNK - 1:CHUNK, :] - b_col3).reshape(tc, LANES)
    beta = betac_ref[...]
    va_task(1)

    def conv(c0, width):
        cols = slice(c0, c0 + width)
        full = ubuf_ref[:, cols]
        acc = full[HIST:] * wconv_ref[CONV_W - 1:CONV_W, cols]
        for i in range(CONV_W - 1):
            back = CONV_W - 1 - i
            acc = acc + pltpu.roll(full, back, 0)[HIST:] * wconv_ref[i:i + 1, cols]
        return _silu(acc)

    def gdn_prep(jp):
        cq, ck, cv = (conv(j * d + jp * pair_w, pair_w) for j in range(3))
        for hh in range(2):
            h = 2 * jp + hh
            cols = slice(h * GDN_DK, (h + 1) * GDN_DK)
            loc = slice(hh * GDN_DK, (hh + 1) * GDN_DK)
            be_h = beta[:, BETA_LANE + h:BETA_LANE + h + 1]
            eb_h = e_b[:, AIN_LANE + h:AIN_LANE + h + 1]
            elb_h = e_lb[:, AIN_LANE + h:AIN_LANE + h + 1]
            qh = cq[:, loc]
            qh = mrow(qh * lax.rsqrt(jnp.sum(qh * qh, axis=-1, keepdims=True) + EPS) * (GDN_DK ** -0.5))
            kh_ = ck[:, loc]
            kh_ = mrow(kh_ * lax.rsqrt(jnp.sum(kh_ * kh_, axis=-1, keepdims=True) + EPS))
            vh = mrow(cv[:, loc])
            qn_ref[:, cols] = qh.astype(BF16)
            kn_ref[:, cols] = kh_.astype(BF16)
            qdec_ref[:, cols] = (qh * eb_h).astype(BF16)
            kdec_ref[:, cols] = (kh_ * elb_h).astype(BF16)
            bv_ref[:, cols] = (be_h * vh).astype(BF16)
            bek_ref[:, cols] = ((be_h * eb_h) * kh_).astype(BF16)

    def gla_gate_task():
        gk = _log_sigmoid(_dot(ps.astype(BF16), wgk_ref[...]) + bgk_ref[...]) * (1.0 / GLA_GATE_NORM)
        gk = mrow(gk)
        kh, kl = _split2(gk)
        ba_ref[...] = _dot(bd, kh) + _dot(bd, kl)

    def gla_qk_task(which):
        b_a = ba_ref[...].reshape(nch, CHUNK, qk_a)
        b_mid = b_a[:, CHUNK // 2 - 1:CHUNK // 2, :]
        b_last = b_a[:, CHUNK - 1:CHUNK, :]
        if which == 0:
            bdec_ref[...] = jnp.exp(b_last)
            qa = mrow(proj(o_qa, qk_a) * (dk_a ** -0.5)).reshape(nch, CHUNK, qk_a)
            qe_ref[...] = (qa * jnp.exp(b_a - b_mid)).reshape(tc, qk_a).astype(BF16)
            qd_ref[...] = (qa * jnp.exp(b_a)).reshape(tc, qk_a).astype(BF16)
        else:
            ka = mrow(proj(o_ka, qk_a)).reshape(nch, CHUNK, qk_a)
            ke_ref[...] = (ka * jnp.exp(b_mid - b_a)).reshape(tc, qk_a).astype(BF16)
            kd_ref[...] = (ka * jnp.exp(b_last - b_a)).reshape(tc, qk_a).astype(BF16)

    def decay_rows_task():
        sel = (lax.broadcasted_iota(jnp.int32, (GDN_HEADS, LANES), 1)
               == lax.broadcasted_iota(jnp.int32, (GDN_HEADS, LANES), 0) + AIN_LANE).astype(BF16)
        for c in range(nch):
            terms = _split3(bcol_ref[c * CHUNK:(c + 1) * CHUNK, :])
            b_row = sum(_dot_nt(sel, jnp.concatenate([term, term], axis=0)) for term in terms)
            brow_ref[c] = b_row
            dvec_ref[c] = jnp.broadcast_to(jnp.exp(b_row[:, CHUNK - 1:CHUNK]), (GDN_HEADS, LANES))

    side_tasks = [decay_rows_task, gla_gate_task,
                  functools.partial(gla_qk_task, 0), functools.partial(gla_qk_task, 1),
                  functools.partial(silu_task, sza_ref, o_za, 0), functools.partial(silu_task, sza_ref, o_za, 1),
                  functools.partial(silu_task, szb_ref, o_zb, 0), functools.partial(silu_task, szb_ref, o_zb, 1)]

    def side(n=1):
        for _ in range(n):
            if side_tasks:
                side_tasks.pop(0)()

    n_pairs = GDN_HEADS // 2
    for jp in range(n_pairs):
        if jp + 1 < n_pairs:
            gdn_proj(jp + 1)
        side(SIDE_PER_PAIR)
        gdn_prep(jp)

    tail = ubuf_ref[HIST + n_valid - (CONV_W - 1):HIST + n_valid, :]
    ubuf_ref[HIST - (CONV_W - 1):HIST, :] = tail
    conv_ref[0] = tail

    npair = GDN_HEADS // 2
    ri = lax.broadcasted_iota(jnp.int32, (CHUNK, 2 * CHUNK), 0)
    li = lax.broadcasted_iota(jnp.int32, (CHUNK, 2 * CHUNK), 1)
    ci = li & (CHUNK - 1)
    lo = li < CHUNK
    incl = ri >= ci
    strict = ri > ci
    eye = jnp.where(ri == ci, 1.0, 0.0).astype(F32)
    blk = {}
    s_ = INV_BASE
    while s_ <= CHUNK:
        sh = s_.bit_length() - 1
        blk[s_] = (ri >> sh) == (ci >> sh)
        s_ *= 2
    incl1 = (lax.broadcasted_iota(jnp.int32, (CHUNK, CHUNK), 0)
             >= lax.broadcasted_iota(jnp.int32, (CHUNK, CHUNK), 1))
    gna = gna_ref[...]
    gnb = gnb_ref[...]
    zblk = jnp.zeros((CHUNK, LANES), BF16)

    def bdiag_packed(y):
        return jnp.concatenate([jnp.where(lo, y, 0), jnp.where(lo, 0, y)], axis=0)

    def bdiag_wide(y):
        return jnp.concatenate([jnp.concatenate([y[:, 0:LANES], zblk], axis=1),
                                jnp.concatenate([zblk, y[:, LANES:2 * LANES]], axis=1)], axis=0)

    def chunk_rows(c):
        return pl.ds(pl.multiple_of(c * CHUNK, CHUNK), CHUNK)

    chains = [(c, p) for c in range(nch) for p in range(npair)]
    crow = [slice(c * CHUNK, (c + 1) * CHUNK) for c in range(nch)]
    pcols = lambda p: slice(p * pair_w, (p + 1) * pair_w)
    kbds = [bdiag_wide(kn_ref[crow[c], pcols(p)]) for c, p in chains]
    kks = [_dot_nt(kn_ref[crow[c], pcols(p)], kbd) for (c, p), kbd in zip(chains, kbds)]
    qks = [_dot_nt(qn_ref[crow[c], pcols(p)], kbd) for (c, p), kbd in zip(chains, kbds)]
    bcs = [bcol_ref[crow[c], :] for c in range(nch)]
    bes = [betac_ref[crow[c], :] for c in range(nch)]
    brs = [brow_ref[c] for c in range(nch)]
    side()
    a_s, tinvs, pws = [], [], []
    for n_, (c, p) in enumerate(chains):
        h1, h2 = 2 * p, 2 * p + 1
        pick = lambda v, l0: jnp.where(lo, v[:, l0 + h1:l0 + h1 + 1], v[:, l0 + h2:l0 + h2 + 1])
        diff = pick(bcs[c], AIN_LANE) - jnp.where(lo[0:1], brs[c][h1:h1 + 1, :], brs[c][h2:h2 + 1, :])
        dm = jnp.where(incl, jnp.exp(jnp.where(incl, diff, 0.0)), 0.0)
        qkm_ref[c, p] = (qks[n_] * dm).astype(BF16)
        a = jnp.where(strict, pick(bes[c], BETA_LANE) * kks[n_] * dm, 0.0)
        dblk = jnp.where(blk[INV_BASE], a, 0.0)
        a_s.append(a)
        tinvs.append(eye - dblk)
        pws.append(dblk.astype(BF16))
    side()
    n = 2
    while n < INV_BASE:
        pws = [_dot(p_, bdiag_packed(p_)).astype(BF16) for p_ in pws]
        tinvs = [t_ + _dot(t_.astype(BF16), bdiag_packed(p_)) for t_, p_ in zip(tinvs, pws)]
        side()
        n *= 2
    while n < CHUNK:
        es = [jnp.where(blk[2 * n] & ~blk[n], a, 0.0).astype(BF16) for a in a_s]
        tbs = [t_.astype(BF16) for t_ in tinvs]
        tbds = [bdiag_packed(tb) for tb in tbs]
        tes = [_dot(tb, bdiag_packed(e)).astype(BF16) for tb, e in zip(tbs, es)]
        side()
        tinvs = [t_ - _dot(te, tbd) for t_, te, tbd in zip(tinvs, tes, tbds)]
        side()
        n *= 2
    for t_, (c, p) in zip(tinvs, chains):
        tb = t_.astype(BF16)
        uv_ref[crow[c], pcols(p)] = _dot(tb, bdiag_wide(bv_ref[crow[c], pcols(p)]))
        wk_ref[crow[c], pcols(p)] = _dot(tb, bdiag_wide(bek_ref[crow[c], pcols(p)])).astype(BF16)
    side(len(side_tasks))

    def chunk_body(c, carry):
        rows = chunk_rows(c)
        ha, hb_ = range(GLA_HEADS), range(GDN_HEADS)
        lk = lambda h: slice(h * dk_a, (h + 1) * dk_a)
        lv = lambda h: slice(h * dv_a, (h + 1) * dv_a)
        hc = lambda h: slice(h * GDN_DK, (h + 1) * GDN_DK)
        sa = [sgla_ref[0, h] for h in ha]
        sb = [sgdn_ref[0, h] for h in hb_]
        v_a = [va_ref[rows, lv(h)] for h in ha]
        dec_a = bdec_ref[c]
        dv_all = dvec_ref[c]
        sab = [s_.astype(BF16) for s_ in sa]
        sbb = [s_.astype(BF16) for s_ in sb]
        wq = [_dot(jnp.concatenate([wk_ref[rows, hc(h)], qdec_ref[rows, hc(h)]], axis=0), sbb[h]) for h in hb_]
        ws = [w_[0:CHUNK] for w_ in wq]
        qsb = [w_[CHUNK:2 * CHUNK] for w_ in wq]
        att = [jnp.where(incl1, _dot_nt(qe_ref[rows, lk(h)], ke_ref[rows, lk(h)]), 0.0).astype(BF16) for h in ha]
        oi = [_dot(qd_ref[rows, lk(h)], sab[h]) for h in ha]
        kv = [_dot_tn(kd_ref[rows, lk(h)], v_a[h]) for h in ha]
        u = [(uv_ref[rows, hc(h)] - ws[h]).astype(BF16) for h in hb_]
        qku = [_dot(qkm_ref[c, p], bdiag_wide(jnp.concatenate([u[2 * p], u[2 * p + 1]], axis=1)))
               for p in range(npair)]
        o_b = [qsb[h] + qku[h // 2][:, (h % 2) * GDN_DV:(h % 2 + 1) * GDN_DV] for h in hb_]
        sb_new = [dv_all[h:h + 1, :] * sb[h] + _dot_tn(kdec_ref[rows, hc(h)], u[h]) for h in hb_]
        o_a = [_dot(att[h], v_a[h]) + oi[h] for h in ha]
        sa_new = []
        for h in ha:
            dcol = jnp.broadcast_to(dec_a[:, lk(h)], (dk_a, dk_a)).T
            sa_new.append(jnp.concatenate([dcol] * (dv_a // dk_a), axis=1) * sa[h] + kv[h])
        for h in ha:
            o = o_a[h]
            o = o * lax.rsqrt(jnp.mean(o * o, axis=-1, keepdims=True) + EPS) * gna
            oa_ref[rows, lv(h)] = (o * sza_ref[rows, lv(h)]).astype(BF16)
            sgla_ref[0, h] = sa_new[h]
        for h in hb_:
            o = o_b[h]
            o = o * lax.rsqrt(jnp.mean(o * o, axis=-1, keepdims=True) + EPS) * gnb
            ob_ref[rows, hc(h)] = (o * szb_ref[rows, hc(h)]).astype(BF16)
            sgdn_ref[0, h] = sb_new[h]
        return carry

    lax.fori_loop(0, nch, chunk_body, 0)

    merged = (jax.nn.sigmoid(proj(o_ga, d)) * _dot(oa_ref[...], wpa_ref[...])
              + jax.nn.sigmoid(proj(o_gb, d)) * _dot(ob_ref[...], wpb_ref[...]))
    out = _dot(merged.astype(BF16), wout_ref[...])
    xn = x_ref[0] + mod_ref[0][:, 2 * d:3 * d] * out
    if final_norm:
        xn = xn * lax.rsqrt(jnp.mean(xn * xn, axis=-1, keepdims=True) + EPS) * gfin_ref[...]
    y_ref[0] = xn


def _resident(shape):
    zeros = (0,) * len(shape)
    return pl.BlockSpec(shape, lambda b, t: zeros, pipeline_mode=pl.Buffered(1))


def _layer(x, mod, consts, states, *, tc, n_valid, final_norm):
    bsz, t_len, d = x.shape
    nch = tc // CHUNK
    qk_a = d // 2
    has_state = states is not None
    const_specs = [_resident(c.shape) for c in consts]
    in_specs = [pl.BlockSpec((1, tc, d), lambda b, t: (b, t, 0)),
                pl.BlockSpec((1, 1, 3 * d), lambda b, t: (b, 0, 0))] + const_specs
    args = [x, mod.reshape(bsz, 1, 3 * d)] + list(consts)
    state_shapes = [(bsz, GLA_HEADS, qk_a // GLA_HEADS, d // GLA_HEADS),
                    (bsz, GDN_HEADS, GDN_DK, GDN_DV),
                    (bsz, CONV_W - 1, 3 * d)]
    state_specs = [pl.BlockSpec((1,) + s[1:], lambda b, t, n=len(s): (b,) + (0,) * (n - 1)) for s in state_shapes]
    if has_state:
        in_specs += state_specs
        args += list(states)
    scratch = [
        pltpu.VMEM((tc, d), BF16),
        pltpu.VMEM((tc, qk_a), BF16), pltpu.VMEM((tc, qk_a), BF16),
        pltpu.VMEM((tc, qk_a), BF16), pltpu.VMEM((tc, qk_a), BF16),
        pltpu.VMEM((tc, d), BF16),
        pltpu.VMEM((tc, d), F32),
        pltpu.VMEM((nch, 1, qk_a), F32),
        pltpu.VMEM((HIST + tc, 3 * d), F32),
        pltpu.VMEM((tc, d), BF16), pltpu.VMEM((tc, d), BF16),
        pltpu.VMEM((tc, d), BF16), pltpu.VMEM((tc, d), BF16),
        pltpu.VMEM((tc, d), BF16), pltpu.VMEM((tc, d), BF16),
        pltpu.VMEM((tc, d), F32),
        pltpu.VMEM((tc, LANES), F32), pltpu.VMEM((tc, LANES), F32),
        pltpu.VMEM((nch, GDN_HEADS, 2 * CHUNK), F32),
        pltpu.VMEM((nch, GDN_HEADS, LANES), F32),
        pltpu.VMEM((tc, d), F32), pltpu.VMEM((tc, d), BF16),
        pltpu.VMEM((nch, GDN_HEADS // 2, CHUNK, 2 * CHUNK), BF16),
        pltpu.VMEM((tc, qk_a), F32),
        pltpu.VMEM((tc, d), BF16), pltpu.VMEM((tc, d), BF16),
    ]
    kern = functools.partial(_layer_kernel, tc=tc, n_valid=n_valid, has_state=has_state,
                             final_norm=final_norm, d_model=d)
    return pl.pallas_call(
        kern,
        grid=(bsz, t_len // tc),
        in_specs=in_specs,
        out_specs=[pl.BlockSpec((1, tc, d), lambda b, t: (b, t, 0))] + state_specs,
        out_shape=[jax.ShapeDtypeStruct(x.shape, F32)] + [jax.ShapeDtypeStruct(s, F32) for s in state_shapes],
        scratch_shapes=scratch,
        compiler_params=pltpu.CompilerParams(dimension_semantics=("arbitrary", "arbitrary"),
                                             vmem_limit_bytes=VMEM_LIMIT_BYTES),
        name="gla_gdn_layer",
    )(*args)


def _layer_consts(tc, g_norm1, w_in, w_gk2, b_gk, w_conv, a_log, dt_bias, g_norm_a, g_norm_b,
                  w_pa, w_pb, w_out, g_final):
    d = w_in.shape[0]
    qk_a = d // 2
    o_gk = 2 * qk_a + 2 * d
    o_qkv = o_gk + GLA_RANK
    o_zb = o_qkv + 3 * d
    o_beta = o_zb + d
    o_a = o_beta + GDN_HEADS
    o_ga = o_a + GDN_HEADS
    pad = LANES - (GLA_RANK + 2 * GDN_HEADS)
    w_a = w_in[:, 0:o_gk].astype(BF16)
    w_b = w_in[:, o_qkv:o_beta].astype(BF16)
    w_c = w_in[:, o_ga:o_ga + 2 * d].astype(BF16)
    w_sm = jnp.concatenate([w_in[:, o_gk:o_qkv], w_in[:, o_beta:o_ga], jnp.zeros((d, pad), w_in.dtype)],
                           axis=1).astype(BF16)
    w_gk = jnp.zeros((LANES, qk_a), F32).at[0:GLA_RANK].set(w_gk2).astype(BF16)
    lane_vec = lambda v: jnp.zeros((1, LANES), F32).at[0, AIN_LANE:AIN_LANE + GDN_HEADS].set(v)
    tt = jnp.arange(tc)
    bd = ((tt[:, None] // CHUNK == tt[None, :] // CHUNK) & (tt[None, :] <= tt[:, None])).astype(BF16)
    return [g_norm1.reshape(1, d), w_a, w_b, w_c, w_sm, w_gk, b_gk.reshape(1, qk_a), w_conv,
            lane_vec(a_log), lane_vec(dt_bias),
            g_norm_a.reshape(1, -1), g_norm_b.reshape(1, -1),
            w_pa.astype(BF16), w_pb.astype(BF16), w_out.astype(BF16), g_final.reshape(1, d), bd]


PROMPT_TILE = 256


def kernel(x_prompt, x_sample, c_prompt, c_sample, state_gla, state_gdn, cache_conv_gdn, w_ada, b_ada, g_norm1, w_in, w_gk2, b_gk, w_conv, a_log, dt_bias, g_norm_a, g_norm_b, w_pa, w_pb, w_out, g_final):
    depth = w_in.shape[0]
    bp, tp, _ = x_prompt.shape
    bs, ts, _ = x_sample.shape
    assert tp % PROMPT_TILE == 0 and CONV_W - 1 <= ts <= CHUNK
    hp = x_prompt
    hs = jnp.pad(x_sample, ((0, 0), (0, CHUNK - ts), (0, 0)))
    outs_p, outs_s = [], []
    for layer in range(depth):
        last = layer == depth - 1
        mod = _adaln_mod(jnp.concatenate([c_prompt, c_sample], axis=0), w_ada[layer], b_ada[layer])
        lw = (g_norm1[layer], w_in[layer], w_gk2[layer], b_gk[layer], w_conv[layer], a_log[layer],
              dt_bias[layer], g_norm_a[layer], g_norm_b[layer], w_pa[layer], w_pb[layer], w_out[layer], g_final)
        hp, *st_p = _layer(hp, mod[:bp], _layer_consts(PROMPT_TILE, *lw), None,
                           tc=PROMPT_TILE, n_valid=PROMPT_TILE, final_norm=last)
        st_in = (state_gla[layer], state_gdn[layer], cache_conv_gdn[layer])
        hs, *st_s = _layer(hs, mod[bp:], _layer_consts(CHUNK, *lw), st_in,
                           tc=CHUNK, n_valid=ts, final_norm=last)
        outs_p.append(st_p)
        outs_s.append(st_s)

    stack = lambda outs, i: jnp.stack([o[i] for o in outs])
    return (hp, hs[:, :ts], stack(outs_p, 0), stack(outs_p, 1), stack(outs_p, 2),
            stack(outs_s, 0), stack(outs_s, 1), stack(outs_s, 2))
```

```python
import functools

import jax
import jax.numpy as jnp
from jax import lax
from jax.experimental import pallas as pl
from jax.experimental.pallas import tpu as pltpu

F32 = jnp.float32
BF16 = jnp.bfloat16

CHUNK = 64
EPS = 1e-6
GLA_HEADS = 4
GLA_RANK = 16
GLA_GATE_NORM = 16.0
GDN_HEADS = 8
GDN_DK = 128
GDN_DV = 128
CONV_W = 4
LANES = 128
HIST = 8
INV_BASE = 8
PREP_ROWS = 128
BETA_LANE = GLA_RANK
AIN_LANE = GLA_RANK + GDN_HEADS
VMEM_LIMIT_BYTES = 56 * 1024 * 1024


def _dot(a, b):
    return jnp.dot(a, b, preferred_element_type=F32)


def _dot_nt(a, b):
    return lax.dot_general(a, b, (((1,), (1,)), ((), ())), preferred_element_type=F32)


def _dot_tn(a, b):
    return lax.dot_general(a, b, (((0,), (0,)), ((), ())), preferred_element_type=F32)


def _split2(x):
    hi = x.astype(BF16)
    return hi, (x - hi.astype(F32)).astype(BF16)


def _split3(x):
    hi = x.astype(BF16)
    r = x - hi.astype(F32)
    mid = r.astype(BF16)
    lo = (r - mid.astype(F32)).astype(BF16)
    return hi, mid, lo


def _softplus(x):
    return jnp.maximum(x, 0.0) + jnp.log1p(jnp.exp(-jnp.abs(x)))


def _log_sigmoid(x):
    return jnp.minimum(x, 0.0) - jnp.log1p(jnp.exp(-jnp.abs(x)))


def _silu(x):
    return x * jax.nn.sigmoid(x)


def _mod_kernel(c_ref, w_ref, b_ref, o_ref):
    s = _silu(c_ref[...]).astype(BF16)
    o_ref[...] = _dot(s, w_ref[...].astype(BF16)) + b_ref[...]


def _adaln_mod(c, w_ada, b_ada):
    n, d = c.shape
    d3 = w_ada.shape[1]
    bn = 512
    return pl.pallas_call(
        _mod_kernel,
        grid=(d3 // bn,),
        in_specs=[pl.BlockSpec((n, d), lambda j: (0, 0)),
                  pl.BlockSpec((d, bn), lambda j: (0, j)),
                  pl.BlockSpec((1, bn), lambda j: (0, j))],
        out_specs=pl.BlockSpec((n, bn), lambda j: (0, j)),
        out_shape=jax.ShapeDtypeStruct((n, d3), F32),
        name="adaln_mod",
    )(c, w_ada, b_ada.reshape(1, d3))


def _layer_kernel(*refs, tc, n_valid, has_state, final_norm, d_model):
    d = d_model
    qk_a = d // 2
    dk_a = qk_a // GLA_HEADS
    dv_a = d // GLA_HEADS
    nch = tc // CHUNK
    o_qa, o_ka, o_va, o_za = 0, qk_a, 2 * qk_a, 2 * qk_a + d
    o_qb = o_za + d
    o_kb, o_vb = o_qb + d, o_qb + 2 * d
    o_zb = o_qb + 3 * d
    o_ga, o_gb = o_zb + d, o_zb + 2 * d
    o_sm = o_gb + d

    it = iter(refs)
    x_ref, mod_ref, g1_ref, win_ref, wat_ref, wgk_ref, bgk_ref, wconv_ref = (next(it) for _ in range(8))
    alane_ref, dlane_ref, acol_ref, dcol_ref = (next(it) for _ in range(4))
    gna_ref, gnb_ref, wpa_ref, wpb_ref, wout_ref, gfin_ref, bd_ref, ut_ref = (next(it) for _ in range(8))
    if has_state:
        sgla_in, sgdn_in, conv_in = (next(it) for _ in range(3))
    y_ref, sgla_ref, sgdn_ref, conv_ref = (next(it) for _ in range(4))
    (hb_ref, qe_ref, ke_ref, qd_ref, kd_ref, va_ref, sza_ref, bdec_ref, ubuf_ref,
     qn_ref, kn_ref, qdec_ref, kdec_ref, bv_ref, bek_ref, szb_ref,
     bcol_ref, betac_ref, brow_ref, dvec_ref, uv_ref, wk_ref, qkm_ref, ba_ref, oa_ref, ob_ref) = (next(it) for _ in range(26))

    t = pl.program_id(1)

    @pl.when(t == 0)
    def _init():
        if has_state:
            sgla_ref[...] = sgla_in[...]
            sgdn_ref[...] = sgdn_in[...]
            ubuf_ref[HIST - (CONV_W - 1):HIST, :] = conv_in[0]
        else:
            sgla_ref[...] = jnp.zeros_like(sgla_ref)
            sgdn_ref[...] = jnp.zeros_like(sgdn_ref)
            ubuf_ref[0:HIST, :] = jnp.zeros((HIST, ubuf_ref.shape[1]), F32)

    masked = n_valid < tc
    if masked:
        rowmask = lax.broadcasted_iota(jnp.int32, (tc, 1), 0) < n_valid

    def mrow(v):
        return jnp.where(rowmask, v, 0.0) if masked else v

    x = x_ref[0]
    mod = mod_ref[0]
    shift, scale = mod[:, 0:d], mod[:, d:2 * d]
    hn = x * lax.rsqrt(jnp.mean(x * x, axis=-1, keepdims=True) + EPS) * g1_ref[...]
    hb_ref[...] = (hn * (1.0 + scale) + shift).astype(BF16)

    def proj(c0, width):
        return _dot(hb_ref[...], win_ref[:, c0:c0 + width])

    pair_w = 2 * GDN_DK
    half = d // 2

    def gdn_proj(jp, parts=(0, 1, 2)):
        for j in parts:
            c0 = j * d + jp * pair_w
            ubuf_ref[HIST:HIST + tc, c0:c0 + pair_w] = proj(o_qb + c0, pair_w)

    def va_task(i):
        va_ref[:, i * half:(i + 1) * half] = mrow(proj(o_va + i * half, half)).astype(BF16)

    def silu_task(dst_ref, c0, i):
        dst_ref[:, i * half:(i + 1) * half] = _silu(proj(c0 + i * half, half))

    ps = proj(o_sm, LANES)
    arows = [_dot_nt(wat_ref[...], hb_ref[c * CHUNK:(c + 1) * CHUNK, :]) for c in range(nch)]
    gdn_proj(0, (0,))
    lane = lax.broadcasted_iota(jnp.int32, (1, LANES), 1)
    betac_ref[...] = mrow(jax.nn.sigmoid(ps))
    g_col = -jnp.exp(alane_ref[...]) * _softplus(ps + dlane_ref[...])
    g_col = mrow(jnp.where((lane >= AIN_LANE) & (lane < AIN_LANE + GDN_HEADS), g_col, 0.0))
    g_rows = []
    for c in range(nch):
        g_row = -jnp.exp(acol_ref[...]) * _softplus(arows[c] + dcol_ref[...])
        if masked:
            colmask = (lax.broadcasted_iota(jnp.int32, (1, CHUNK), 1) + c * CHUNK) < n_valid
            g_row = jnp.where(colmask, g_row, 0.0)
        g_rows.append(g_row)
    gdn_proj(0, (1,))
    bd = bd_ref[...]
    ut = ut_ref[...]
    gh, gm, gl = _split3(g_col)
    b_col = _dot(bd, gh) + _dot(bd, gm) + _dot(bd, gl)
    b_rows = []
    for c in range(nch):
        rh, rm, rl = _split3(g_rows[c])
        b_rows.append(_dot(rh, ut) + _dot(rm, ut) + _dot(rl, ut))
    gdn_proj(0, (2,))
    va_task(0)
    bcol_ref[...] = b_col
    for c in range(nch):
        brow_ref[c] = b_rows[c]
        dvec_ref[c] = jnp.broadcast_to(jnp.exp(b_rows[c][:, CHUNK - 1:CHUNK]), (GDN_HEADS, LANES))
    e_b = jnp.exp(b_col)
    b_col3 = b_col.reshape(nch, CHUNK, LANES)
    e_lb = jnp.exp(b_col3[:, CHUNK - 1:CHUNK, :] - b_col3).reshape(tc, LANES)
    beta = betac_ref[...]
    va_task(1)

    def conv(c0, width):
        cols = slice(c0, c0 + width)
        full = ubuf_ref[:, cols]
        acc = full[HIST:] * wconv_ref[CONV_W - 1:CONV_W, cols]
        for i in range(CONV_W - 1):
            back = CONV_W - 1 - i
            acc = acc + pltpu.roll(full, back, 0)[HIST:] * wconv_ref[i:i + 1, cols]
        return _silu(acc)

    def gdn_prep(h):
        cols = slice(h * GDN_DK, (h + 1) * GDN_DK)
        qh, kh_, vh = (conv(j * d + h * GDN_DK, GDN_DK) for j in range(3))
        be_h = beta[:, BETA_LANE + h:BETA_LANE + h + 1]
        eb_h = e_b[:, AIN_LANE + h:AIN_LANE + h + 1]
        elb_h = e_lb[:, AIN_LANE + h:AIN_LANE + h + 1]
        qh = mrow(qh * lax.rsqrt(jnp.sum(qh * qh, axis=-1, keepdims=True) + EPS) * (GDN_DK ** -0.5))
        kh_ = mrow(kh_ * lax.rsqrt(jnp.sum(kh_ * kh_, axis=-1, keepdims=True) + EPS))
        vh = mrow(vh)
        qn_ref[:, cols] = qh.astype(BF16)
        kn_ref[:, cols] = kh_.astype(BF16)
        qdec_ref[:, cols] = (qh * eb_h).astype(BF16)
        kdec_ref[:, cols] = (kh_ * elb_h).astype(BF16)
        bv_ref[:, cols] = (be_h * vh).astype(BF16)
        bek_ref[:, cols] = ((be_h * eb_h) * kh_).astype(BF16)

    def gla_gate_task():
        gk = _log_sigmoid(_dot(ps.astype(BF16), wgk_ref[...]) + bgk_ref[...]) * (1.0 / GLA_GATE_NORM)
        gk = mrow(gk)
        kh, kl = _split2(gk)
        ba_ref[...] = _dot(bd, kh) + _dot(bd, kl)

    def gla_qk_task(which):
        b_a = ba_ref[...].reshape(nch, CHUNK, qk_a)
        b_mid = b_a[:, CHUNK // 2 - 1:CHUNK // 2, :]
        b_last = b_a[:, CHUNK - 1:CHUNK, :]
        if which == 0:
            bdec_ref[...] = jnp.exp(b_last)
            qa = mrow(proj(o_qa, qk_a) * (dk_a ** -0.5)).reshape(nch, CHUNK, qk_a)
            qe_ref[...] = (qa * jnp.exp(b_a - b_mid)).reshape(tc, qk_a).astype(BF16)
            qd_ref[...] = (qa * jnp.exp(b_a)).reshape(tc, qk_a).astype(BF16)
        else:
            ka = mrow(proj(o_ka, qk_a)).reshape(nch, CHUNK, qk_a)
            ke_ref[...] = (ka * jnp.exp(b_mid - b_a)).reshape(tc, qk_a).astype(BF16)
            kd_ref[...] = (ka * jnp.exp(b_last - b_a)).reshape(tc, qk_a).astype(BF16)

    side_tasks = [gla_gate_task,
                  functools.partial(gla_qk_task, 0), functools.partial(gla_qk_task, 1),
                  functools.partial(silu_task, sza_ref, o_za, 0), functools.partial(silu_task, sza_ref, o_za, 1),
                  functools.partial(silu_task, szb_ref, o_zb, 0), functools.partial(silu_task, szb_ref, o_zb, 1)]

    def side(n=1):
        for _ in range(n):
            if side_tasks:
                side_tasks.pop(0)()

    n_pairs = GDN_HEADS // 2
    for jp in range(n_pairs):
        more = jp + 1 < n_pairs
        if more:
            gdn_proj(jp + 1, (0,))
        gdn_prep(2 * jp)
        if more:
            gdn_proj(jp + 1, (1,))
        side()
        gdn_prep(2 * jp + 1)
        if more:
            gdn_proj(jp + 1, (2,))
        side()

    tail = ubuf_ref[HIST + n_valid - (CONV_W - 1):HIST + n_valid, :]
    ubuf_ref[HIST - (CONV_W - 1):HIST, :] = tail
    conv_ref[0] = tail

    npair = GDN_HEADS // 2
    ri = lax.broadcasted_iota(jnp.int32, (CHUNK, 2 * CHUNK), 0)
    li = lax.broadcasted_iota(jnp.int32, (CHUNK, 2 * CHUNK), 1)
    ci = li & (CHUNK - 1)
    lo = li < CHUNK
    incl = ri >= ci
    strict = ri > ci
    eye = jnp.where(ri == ci, 1.0, 0.0).astype(F32)
    blk = {}
    s_ = INV_BASE
    while s_ <= CHUNK:
        sh = s_.bit_length() - 1
        blk[s_] = (ri >> sh) == (ci >> sh)
        s_ *= 2
    incl1 = (lax.broadcasted_iota(jnp.int32, (CHUNK, CHUNK), 0)
             >= lax.broadcasted_iota(jnp.int32, (CHUNK, CHUNK), 1))
    gna = gna_ref[...]
    gnb = gnb_ref[...]
    zblk = jnp.zeros((CHUNK, LANES), BF16)

    def bdiag_packed(y):
        return jnp.concatenate([jnp.where(lo, y, 0), jnp.where(lo, 0, y)], axis=0)

    def bdiag_wide(y):
        return jnp.concatenate([jnp.concatenate([y[:, 0:LANES], zblk], axis=1),
                                jnp.concatenate([zblk, y[:, LANES:2 * LANES]], axis=1)], axis=0)

    def chunk_rows(c):
        return pl.ds(pl.multiple_of(c * CHUNK, CHUNK), CHUNK)

    chains = [(c, p) for c in range(nch) for p in range(npair)]
    crow = [slice(c * CHUNK, (c + 1) * CHUNK) for c in range(nch)]
    pcols = lambda p: slice(p * pair_w, (p + 1) * pair_w)
    kbds = [bdiag_wide(kn_ref[crow[c], pcols(p)]) for c, p in chains]
    kks = [_dot_nt(kn_ref[crow[c], pcols(p)], kbd) for (c, p), kbd in zip(chains, kbds)]
    qks = [_dot_nt(qn_ref[crow[c], pcols(p)], kbd) for (c, p), kbd in zip(chains, kbds)]
    bcs = [bcol_ref[crow[c], :] for c in range(nch)]
    bes = [betac_ref[crow[c], :] for c in range(nch)]
    brs = [brow_ref[c] for c in range(nch)]
    side()
    a_s, tinvs, pws = [], [], []
    for n_, (c, p) in enumerate(chains):
        h1, h2 = 2 * p, 2 * p + 1
        pick = lambda v, l0: jnp.where(lo, v[:, l0 + h1:l0 + h1 + 1], v[:, l0 + h2:l0 + h2 + 1])
        diff = pick(bcs[c], AIN_LANE) - jnp.where(lo[0:1], brs[c][h1:h1 + 1, :], brs[c][h2:h2 + 1, :])
        dm = jnp.where(incl, jnp.exp(jnp.where(incl, diff, 0.0)), 0.0)
        qkm_ref[c, p] = (qks[n_] * dm).astype(BF16)
        a = jnp.where(strict, pick(bes[c], BETA_LANE) * kks[n_] * dm, 0.0)
        dblk = jnp.where(blk[INV_BASE], a, 0.0)
        a_s.append(a)
        tinvs.append(eye - dblk)
        pws.append(dblk.astype(BF16))
    side()
    n = 2
    while n < INV_BASE:
        pws = [_dot(p_, bdiag_packed(p_)).astype(BF16) for p_ in pws]
        tinvs = [t_ + _dot(t_.astype(BF16), bdiag_packed(p_)) for t_, p_ in zip(tinvs, pws)]
        side()
        n *= 2
    while n < CHUNK:
        es = [jnp.where(blk[2 * n] & ~blk[n], a, 0.0).astype(BF16) for a in a_s]
        tbs = [t_.astype(BF16) for t_ in tinvs]
        tbds = [bdiag_packed(tb) for tb in tbs]
        tes = [_dot(tb, bdiag_packed(e)).astype(BF16) for tb, e in zip(tbs, es)]
        side()
        tinvs = [t_ - _dot(te, tbd) for t_, te, tbd in zip(tinvs, tes, tbds)]
        side()
        n *= 2
    for t_, (c, p) in zip(tinvs, chains):
        tb = t_.astype(BF16)
        uv_ref[crow[c], pcols(p)] = _dot(tb, bdiag_wide(bv_ref[crow[c], pcols(p)]))
        wk_ref[crow[c], pcols(p)] = _dot(tb, bdiag_wide(bek_ref[crow[c], pcols(p)])).astype(BF16)
    side(len(side_tasks))

    def chunk_body(c, carry):
        rows = chunk_rows(c)
        ha, hb_ = range(GLA_HEADS), range(GDN_HEADS)
        lk = lambda h: slice(h * dk_a, (h + 1) * dk_a)
        lv = lambda h: slice(h * dv_a, (h + 1) * dv_a)
        hc = lambda h: slice(h * GDN_DK, (h + 1) * GDN_DK)
        sa = [sgla_ref[0, h] for h in ha]
        sb = [sgdn_ref[0, h] for h in hb_]
        v_a = [va_ref[rows, lv(h)] for h in ha]
        dec_a = bdec_ref[c]
        dv_all = dvec_ref[c]
        sab = [s_.astype(BF16) for s_ in sa]
        sbb = [s_.astype(BF16) for s_ in sb]
        wq = [_dot(jnp.concatenate([wk_ref[rows, hc(h)], qdec_ref[rows, hc(h)]], axis=0), sbb[h]) for h in hb_]
        ws = [w_[0:CHUNK] for w_ in wq]
        qsb = [w_[CHUNK:2 * CHUNK] for w_ in wq]
        att = [jnp.where(incl1, _dot_nt(qe_ref[rows, lk(h)], ke_ref[rows, lk(h)]), 0.0).astype(BF16) for h in ha]
        oi = [_dot(qd_ref[rows, lk(h)], sab[h]) for h in ha]
        kv = [_dot_tn(kd_ref[rows, lk(h)], v_a[h]) for h in ha]
        u = [(uv_ref[rows, hc(h)] - ws[h]).astype(BF16) for h in hb_]
        qku = [_dot(qkm_ref[c, p], bdiag_wide(jnp.concatenate([u[2 * p], u[2 * p + 1]], axis=1)))
               for p in range(npair)]
        o_b = [qsb[h] + qku[h // 2][:, (h % 2) * GDN_DV:(h % 2 + 1) * GDN_DV] for h in hb_]
        sb_new = [dv_all[h:h + 1, :] * sb[h] + _dot_tn(kdec_ref[rows, hc(h)], u[h]) for h in hb_]
        o_a = [_dot(att[h], v_a[h]) + oi[h] for h in ha]
        sa_new = []
        for h in ha:
            dcol = jnp.broadcast_to(dec_a[:, lk(h)], (dk_a, dk_a)).T
            sa_new.append(jnp.concatenate([dcol] * (dv_a // dk_a), axis=1) * sa[h] + kv[h])
        for h in ha:
            o = o_a[h]
            o = o * lax.rsqrt(jnp.mean(o * o, axis=-1, keepdims=True) + EPS) * gna
            oa_ref[rows, lv(h)] = (o * sza_ref[rows, lv(h)]).astype(BF16)
            sgla_ref[0, h] = sa_new[h]
        for h in hb_:
            o = o_b[h]
            o = o * lax.rsqrt(jnp.mean(o * o, axis=-1, keepdims=True) + EPS) * gnb
            ob_ref[rows, hc(h)] = (o * szb_ref[rows, hc(h)]).astype(BF16)
            sgdn_ref[0, h] = sb_new[h]
        return carry

    lax.fori_loop(0, nch, chunk_body, 0)

    merged = (jax.nn.sigmoid(proj(o_ga, d)) * _dot(oa_ref[...], wpa_ref[...])
              + jax.nn.sigmoid(proj(o_gb, d)) * _dot(ob_ref[...], wpb_ref[...]))
    out = _dot(merged.astype(BF16), wout_ref[...])
    xn = x_ref[0] + mod_ref[0][:, 2 * d:3 * d] * out
    if final_norm:
        xn = xn * lax.rsqrt(jnp.mean(xn * xn, axis=-1, keepdims=True) + EPS) * gfin_ref[...]
    y_ref[0] = xn


def _resident(shape):
    zeros = (0,) * len(shape)
    return pl.BlockSpec(shape, lambda b, t: zeros, pipeline_mode=pl.Buffered(1))


def _layer(x, mod, consts, states, *, tc, n_valid, final_norm):
    bsz, t_len, d = x.shape
    nch = tc // CHUNK
    qk_a = d // 2
    has_state = states is not None
    const_specs = [_resident(c.shape) for c in consts]
    in_specs = [pl.BlockSpec((1, tc, d), lambda b, t: (b, t, 0)),
                pl.BlockSpec((1, 1, 3 * d), lambda b, t: (b, 0, 0))] + const_specs
    args = [x, mod.reshape(bsz, 1, 3 * d)] + list(consts)
    state_shapes = [(bsz, GLA_HEADS, qk_a // GLA_HEADS, d // GLA_HEADS),
                    (bsz, GDN_HEADS, GDN_DK, GDN_DV),
                    (bsz, CONV_W - 1, 3 * d)]
    state_specs = [pl.BlockSpec((1,) + s[1:], lambda b, t, n=len(s): (b,) + (0,) * (n - 1)) for s in state_shapes]
    if has_state:
        in_specs += state_specs
        args += list(states)
    scratch = [
        pltpu.VMEM((tc, d), BF16),
        pltpu.VMEM((tc, qk_a), BF16), pltpu.VMEM((tc, qk_a), BF16),
        pltpu.VMEM((tc, qk_a), BF16), pltpu.VMEM((tc, qk_a), BF16),
        pltpu.VMEM((tc, d), BF16),
        pltpu.VMEM((tc, d), F32),
        pltpu.VMEM((nch, 1, qk_a), F32),
        pltpu.VMEM((HIST + tc, 3 * d), F32),
        pltpu.VMEM((tc, d), BF16), pltpu.VMEM((tc, d), BF16),
        pltpu.VMEM((tc, d), BF16), pltpu.VMEM((tc, d), BF16),
        pltpu.VMEM((tc, d), BF16), pltpu.VMEM((tc, d), BF16),
        pltpu.VMEM((tc, d), F32),
        pltpu.VMEM((tc, LANES), F32), pltpu.VMEM((tc, LANES), F32),
        pltpu.VMEM((nch, GDN_HEADS, 2 * CHUNK), F32),
        pltpu.VMEM((nch, GDN_HEADS, LANES), F32),
        pltpu.VMEM((tc, d), F32), pltpu.VMEM((tc, d), BF16),
        pltpu.VMEM((nch, GDN_HEADS // 2, CHUNK, 2 * CHUNK), BF16),
        pltpu.VMEM((tc, qk_a), F32),
        pltpu.VMEM((tc, d), BF16), pltpu.VMEM((tc, d), BF16),
    ]
    kern = functools.partial(_layer_kernel, tc=tc, n_valid=n_valid, has_state=has_state,
                             final_norm=final_norm, d_model=d)
    return pl.pallas_call(
        kern,
        grid=(bsz, t_len // tc),
        in_specs=in_specs,
        out_specs=[pl.BlockSpec((1, tc, d), lambda b, t: (b, t, 0))] + state_specs,
        out_shape=[jax.ShapeDtypeStruct(x.shape, F32)] + [jax.ShapeDtypeStruct(s, F32) for s in state_shapes],
        scratch_shapes=scratch,
        compiler_params=pltpu.CompilerParams(dimension_semantics=("arbitrary", "arbitrary"),
                                             vmem_limit_bytes=VMEM_LIMIT_BYTES),
        name="gla_gdn_layer",
    )(*args)


def _permute_cast_kernel(w_ref, o_ref, *, segments, pad):
    w = w_ref[...]
    parts = [w[:, a:b] for a, b in segments] + [jnp.zeros((w.shape[0], pad), w.dtype)]
    o_ref[...] = jnp.concatenate(parts, axis=1).astype(o_ref.dtype)


def _permute_cast(w, segments, pad):
    rows, cols = w.shape
    out_cols = sum(b - a for a, b in segments) + pad
    bm = PREP_ROWS
    return pl.pallas_call(
        functools.partial(_permute_cast_kernel, segments=segments, pad=pad),
        grid=(rows // bm,),
        in_specs=[pl.BlockSpec((bm, cols), lambda i: (i, 0))],
        out_specs=pl.BlockSpec((bm, out_cols), lambda i: (i, 0)),
        out_shape=jax.ShapeDtypeStruct((rows, out_cols), BF16),
        name="permute_cast",
    )(w)


def _layer_consts(g_norm1, w_in, w_gk2, b_gk, w_conv, a_log, dt_bias, g_norm_a, g_norm_b,
                  w_pa, w_pb, w_out, g_final):
    d = w_in.shape[0]
    qk_a = d // 2
    o_gk = 2 * qk_a + 2 * d
    o_qkv = o_gk + GLA_RANK
    o_zb = o_qkv + 3 * d
    o_beta = o_zb + d
    o_a = o_beta + GDN_HEADS
    o_ga = o_a + GDN_HEADS
    pad = LANES - (GLA_RANK + 2 * GDN_HEADS)
    w_perm = _permute_cast(w_in, ((0, o_gk), (o_qkv, o_beta), (o_ga, o_ga + 2 * d), (o_gk, o_qkv), (o_beta, o_ga)), pad)
    w_at = w_in[:, o_a:o_ga].T.astype(BF16)
    w_gk = jnp.zeros((LANES, qk_a), F32).at[0:GLA_RANK].set(w_gk2).astype(BF16)
    lane_vec = lambda v: jnp.zeros((1, LANES), F32).at[0, AIN_LANE:AIN_LANE + GDN_HEADS].set(v)
    col_vec = lambda v: jnp.broadcast_to(v.reshape(GDN_HEADS, 1), (GDN_HEADS, CHUNK)).astype(F32)
    tu = jnp.arange(CHUNK)
    ut = jnp.tile(tu[:, None] <= tu[None, :], (1, 2)).astype(BF16)
    return [g_norm1.reshape(1, d), w_perm, w_at, w_gk, b_gk.reshape(1, qk_a), w_conv,
            lane_vec(a_log), lane_vec(dt_bias), col_vec(a_log), col_vec(dt_bias),
            g_norm_a.reshape(1, -1), g_norm_b.reshape(1, -1),
            w_pa.astype(BF16), w_pb.astype(BF16), w_out.astype(BF16), g_final.reshape(1, d), ut]


def _tile_consts(consts, tc):
    tt = jnp.arange(tc)
    bd = ((tt[:, None] // CHUNK == tt[None, :] // CHUNK) & (tt[None, :] <= tt[:, None])).astype(BF16)
    return consts[:-1] + [bd, consts[-1]]


PROMPT_TILE = 256


def kernel(x_prompt, x_sample, c_prompt, c_sample, state_gla, state_gdn, cache_conv_gdn, w_ada, b_ada, g_norm1, w_in, w_gk2, b_gk, w_conv, a_log, dt_bias, g_norm_a, g_norm_b, w_pa, w_pb, w_out, g_final):
    depth = w_in.shape[0]
    bp, tp, _ = x_prompt.shape
    bs, ts, _ = x_sample.shape
    assert tp % PROMPT_TILE == 0 and CONV_W - 1 <= ts <= CHUNK
    hp = x_prompt
    hs = jnp.pad(x_sample, ((0, 0), (0, CHUNK - ts), (0, 0)))
    outs_p, outs_s = [], []
    for layer in range(depth):
        last = layer == depth - 1
        mod = _adaln_mod(jnp.concatenate([c_prompt, c_sample], axis=0), w_ada[layer], b_ada[layer])
        lw = (g_norm1[layer], w_in[layer], w_gk2[layer], b_gk[layer], w_conv[layer], a_log[layer],
              dt_bias[layer], g_norm_a[layer], g_norm_b[layer], w_pa[layer], w_pb[layer], w_out[layer], g_final)
        consts = _layer_consts(*lw)
        hp, *st_p = _layer(hp, mod[:bp], _tile_consts(consts, PROMPT_TILE), None,
                           tc=PROMPT_TILE, n_valid=PROMPT_TILE, final_norm=last)
        st_in = (state_gla[layer], state_gdn[layer], cache_conv_gdn[layer])
        hs, *st_s = _layer(hs, mod[bp:], _tile_consts(consts, CHUNK), st_in,
                           tc=CHUNK, n_valid=ts, final_norm=last)
        outs_p.append(st_p)
        outs_s.append(st_s)

    stack = lambda outs, i: jnp.stack([o[i] for o in outs])
    return (hp, hs[:, :ts], stack(outs_p, 0), stack(outs_p, 1), stack(outs_p, 2),
            stack(outs_s, 0), stack(outs_s, 1), stack(outs_s, 2))
```

```python
import functools

import jax
import jax.numpy as jnp
from jax import lax
from jax.experimental import pallas as pl
from jax.experimental.pallas import tpu as pltpu

F32 = jnp.float32
BF16 = jnp.bfloat16

CHUNK = 64
EPS = 1e-6
GLA_HEADS = 4
GLA_RANK = 16
GLA_GATE_NORM = 16.0
GDN_HEADS = 8
GDN_DK = 128
GDN_DV = 128
CONV_W = 4
LANES = 128
HIST = 8
INV_BASE = 8
PREP_ROWS = 128
BETA_LANE = GLA_RANK
AIN_LANE = GLA_RANK + GDN_HEADS
VMEM_LIMIT_BYTES = 56 * 1024 * 1024


def _dot(a, b):
    return jnp.dot(a, b, preferred_element_type=F32)


def _dot_nt(a, b):
    return lax.dot_general(a, b, (((1,), (1,)), ((), ())), preferred_element_type=F32)


def _dot_tn(a, b):
    return lax.dot_general(a, b, (((0,), (0,)), ((), ())), preferred_element_type=F32)


def _split2(x):
    hi = x.astype(BF16)
    return hi, (x - hi.astype(F32)).astype(BF16)


def _split3(x):
    hi = x.astype(BF16)
    r = x - hi.astype(F32)
    mid = r.astype(BF16)
    lo = (r - mid.astype(F32)).astype(BF16)
    return hi, mid, lo


def _softplus(x):
    return jnp.maximum(x, 0.0) + jnp.log1p(jnp.exp(-jnp.abs(x)))


def _log_sigmoid(x):
    return jnp.minimum(x, 0.0) - jnp.log1p(jnp.exp(-jnp.abs(x)))


def _silu(x):
    return x * jax.nn.sigmoid(x)


def _mod_kernel(c_ref, w_ref, b_ref, o_ref):
    s = _silu(c_ref[...]).astype(BF16)
    o_ref[...] = _dot(s, w_ref[...].astype(BF16)) + b_ref[...]


def _adaln_mod(c, w_ada, b_ada):
    n, d = c.shape
    d3 = w_ada.shape[1]
    bn = 512
    return pl.pallas_call(
        _mod_kernel,
        grid=(d3 // bn,),
        in_specs=[pl.BlockSpec((n, d), lambda j: (0, 0)),
                  pl.BlockSpec((d, bn), lambda j: (0, j)),
                  pl.BlockSpec((1, bn), lambda j: (0, j))],
        out_specs=pl.BlockSpec((n, bn), lambda j: (0, j)),
        out_shape=jax.ShapeDtypeStruct((n, d3), F32),
        name="adaln_mod",
    )(c, w_ada, b_ada.reshape(1, d3))


def _layer_kernel(*refs, tc, n_valid, has_state, final_norm, d_model):
    d = d_model
    qk_a = d // 2
    dk_a = qk_a // GLA_HEADS
    dv_a = d // GLA_HEADS
    nch = tc // CHUNK
    o_qa, o_ka, o_va, o_za = 0, qk_a, 2 * qk_a, 2 * qk_a + d
    o_qb = o_za + d
    o_kb, o_vb = o_qb + d, o_qb + 2 * d
    o_zb = o_qb + 3 * d
    o_ga, o_gb = o_zb + d, o_zb + 2 * d
    o_sm = o_gb + d

    it = iter(refs)
    x_ref, mod_ref, g1_ref, win_ref, wat_ref, wgk_ref, bgk_ref, wconv_ref = (next(it) for _ in range(8))
    alane_ref, dlane_ref, acol_ref, dcol_ref = (next(it) for _ in range(4))
    gna_ref, gnb_ref, wpa_ref, wpb_ref, wout_ref, gfin_ref, bd_ref, ut_ref = (next(it) for _ in range(8))
    if has_state:
        sgla_in, sgdn_in, conv_in = (next(it) for _ in range(3))
    y_ref, sgla_ref, sgdn_ref, conv_ref = (next(it) for _ in range(4))
    (hb_ref, qe_ref, ke_ref, qd_ref, kd_ref, va_ref, sza_ref, bdec_ref, ubuf_ref,
     qn_ref, kn_ref, qdec_ref, kdec_ref, bv_ref, bek_ref, szb_ref,
     bcol_ref, betac_ref, brow_ref, dvec_ref, uv_ref, wk_ref, qkm_ref, ba_ref, oa_ref, ob_ref) = (next(it) for _ in range(26))

    t = pl.program_id(1)

    @pl.when(t == 0)
    def _init():
        if has_state:
            sgla_ref[...] = sgla_in[...]
            sgdn_ref[...] = sgdn_in[...]
            ubuf_ref[HIST - (CONV_W - 1):HIST, :] = conv_in[0]
        else:
            sgla_ref[...] = jnp.zeros_like(sgla_ref)
            sgdn_ref[...] = jnp.zeros_like(sgdn_ref)
            ubuf_ref[0:HIST, :] = jnp.zeros((HIST, ubuf_ref.shape[1]), F32)

    masked = n_valid < tc
    if masked:
        rowmask = lax.broadcasted_iota(jnp.int32, (tc, 1), 0) < n_valid

    def mrow(v):
        return jnp.where(rowmask, v, 0.0) if masked else v

    x = x_ref[0]
    mod = mod_ref[0]
    shift, scale = mod[:, 0:d], mod[:, d:2 * d]
    hn = x * lax.rsqrt(jnp.mean(x * x, axis=-1, keepdims=True) + EPS) * g1_ref[...]
    hb_ref[...] = (hn * (1.0 + scale) + shift).astype(BF16)

    def proj(c0, width):
        return _dot(hb_ref[...], win_ref[:, c0:c0 + width])

    pair_w = 2 * GDN_DK
    half = d // 2

    def gdn_proj(jp, parts=(0, 1, 2)):
        for j in parts:
            c0 = j * d + jp * pair_w
            ubuf_ref[HIST:HIST + tc, c0:c0 + pair_w] = proj(o_qb + c0, pair_w)

    def va_task(i):
        va_ref[:, i * half:(i + 1) * half] = mrow(proj(o_va + i * half, half)).astype(BF16)

    def silu_task(dst_ref, c0, i):
        dst_ref[:, i * half:(i + 1) * half] = _silu(proj(c0 + i * half, half))

    ps = proj(o_sm, LANES)
    arows = [_dot_nt(wat_ref[...], hb_ref[c * CHUNK:(c + 1) * CHUNK, :]) for c in range(nch)]
    gdn_proj(0, (0,))
    lane = lax.broadcasted_iota(jnp.int32, (1, LANES), 1)
    betac_ref[...] = mrow(jax.nn.sigmoid(ps))
    g_col = -jnp.exp(alane_ref[...]) * _softplus(ps + dlane_ref[...])
    g_col = mrow(jnp.where((lane >= AIN_LANE) & (lane < AIN_LANE + GDN_HEADS), g_col, 0.0))
    g_rows = []
    for c in range(nch):
        g_row = -jnp.exp(acol_ref[...]) * _softplus(arows[c] + dcol_ref[...])
        if masked:
            colmask = (lax.broadcasted_iota(jnp.int32, (1, CHUNK), 1) + c * CHUNK) < n_valid
            g_row = jnp.where(colmask, g_row, 0.0)
        g_rows.append(g_row)
    gdn_proj(0, (1,))
    bd = bd_ref[...]
    ut = ut_ref[...]
    gh, gm, gl = _split3(g_col)
    b_col = _dot(bd, gh) + _dot(bd, gm) + _dot(bd, gl)
    b_rows = []
    for c in range(nch):
        rh, rm, rl = _split3(g_rows[c])
        b_rows.append(_dot(rh, ut) + _dot(rm, ut) + _dot(rl, ut))
    gdn_proj(0, (2,))
    va_task(0)
    bcol_ref[...] = b_col
    for c in range(nch):
        brow_ref[c] = b_rows[c]
        dvec_ref[c] = jnp.broadcast_to(jnp.exp(b_rows[c][:, CHUNK - 1:CHUNK]), (GDN_HEADS, LANES))
    e_b = jnp.exp(b_col)
    b_col3 = b_col.reshape(nch, CHUNK, LANES)
    e_lb = jnp.exp(b_col3[:, CHUNK - 1:CHUNK, :] - b_col3).reshape(tc, LANES)
    beta = betac_ref[...]
    va_task(1)

    def conv(c0, width):
        cols = slice(c0, c0 + width)
        full = ubuf_ref[:, cols]
        acc = full[HIST:] * wconv_ref[CONV_W - 1:CONV_W, cols]
        for i in range(CONV_W - 1):
            back = CONV_W - 1 - i
            acc = acc + pltpu.roll(full, back, 0)[HIST:] * wconv_ref[i:i + 1, cols]
        return _silu(acc)

    def gdn_prep(h):
        cols = slice(h * GDN_DK, (h + 1) * GDN_DK)
        qh, kh_, vh = (conv(j * d + h * GDN_DK, GDN_DK) for j in range(3))
        be_h = beta[:, BETA_LANE + h:BETA_LANE + h + 1]
        eb_h = e_b[:, AIN_LANE + h:AIN_LANE + h + 1]
        elb_h = e_lb[:, AIN_LANE + h:AIN_LANE + h + 1]
        qh = mrow(qh * lax.rsqrt(jnp.sum(qh * qh, axis=-1, keepdims=True) + EPS) * (GDN_DK ** -0.5))
        kh_ = mrow(kh_ * lax.rsqrt(jnp.sum(kh_ * kh_, axis=-1, keepdims=True) + EPS))
        vh = mrow(vh)
        qn_ref[:, cols] = qh.astype(BF16)
        kn_ref[:, cols] = kh_.astype(BF16)
        qdec_ref[:, cols] = (qh * eb_h).astype(BF16)
        kdec_ref[:, cols] = (kh_ * elb_h).astype(BF16)
        bv_ref[:, cols] = (be_h * vh).astype(BF16)
        bek_ref[:, cols] = ((be_h * eb_h) * kh_).astype(BF16)

    def gla_gate_task():
        gk = _log_sigmoid(_dot(ps.astype(BF16), wgk_ref[...]) + bgk_ref[...]) * (1.0 / GLA_GATE_NORM)
        gk = mrow(gk)
        kh, kl = _split2(gk)
        ba_ref[...] = _dot(bd, kh) + _dot(bd, kl)

    def gla_qk_task(which):
        b_a = ba_ref[...].reshape(nch, CHUNK, qk_a)
        b_mid = b_a[:, CHUNK // 2 - 1:CHUNK // 2, :]
        b_last = b_a[:, CHUNK - 1:CHUNK, :]
        if which == 0:
            bdec_ref[...] = jnp.exp(b_last)
            qa = mrow(proj(o_qa, qk_a) * (dk_a ** -0.5)).reshape(nch, CHUNK, qk_a)
            qe_ref[...] = (qa * jnp.exp(b_a - b_mid)).reshape(tc, qk_a).astype(BF16)
            qd_ref[...] = (qa * jnp.exp(b_a)).reshape(tc, qk_a).astype(BF16)
        else:
            ka = mrow(proj(o_ka, qk_a)).reshape(nch, CHUNK, qk_a)
            ke_ref[...] = (ka * jnp.exp(b_mid - b_a)).reshape(tc, qk_a).astype(BF16)
            kd_ref[...] = (ka * jnp.exp(b_last - b_a)).reshape(tc, qk_a).astype(BF16)

    side_tasks = [gla_gate_task,
                  functools.partial(gla_qk_task, 0), functools.partial(gla_qk_task, 1),
                  functools.partial(silu_task, sza_ref, o_za, 0), functools.partial(silu_task, sza_ref, o_za, 1),
                  functools.partial(silu_task, szb_ref, o_zb, 0), functools.partial(silu_task, szb_ref, o_zb, 1)]

    def side(n=1):
        for _ in range(n):
            if side_tasks:
                side_tasks.pop(0)()

    n_pairs = GDN_HEADS // 2
    for jp in range(n_pairs):
        more = jp + 1 < n_pairs
        if more:
            gdn_proj(jp + 1, (0,))
        gdn_prep(2 * jp)
        if more:
            gdn_proj(jp + 1, (1,))
        side()
        gdn_prep(2 * jp + 1)
        if more:
            gdn_proj(jp + 1, (2,))
        side()

    tail = ubuf_ref[HIST + n_valid - (CONV_W - 1):HIST + n_valid, :]
    ubuf_ref[HIST - (CONV_W - 1):HIST, :] = tail
    conv_ref[0] = tail

    npair = GDN_HEADS // 2
    ri = lax.broadcasted_iota(jnp.int32, (CHUNK, 2 * CHUNK), 0)
    li = lax.broadcasted_iota(jnp.int32, (CHUNK, 2 * CHUNK), 1)
    ci = li & (CHUNK - 1)
    lo = li < CHUNK
    incl = ri >= ci
    strict = ri > ci
    eye = jnp.where(ri == ci, 1.0, 0.0).astype(F32)
    blk = {}
    s_ = INV_BASE
    while s_ <= CHUNK:
        sh = s_.bit_length() - 1
        blk[s_] = (ri >> sh) == (ci >> sh)
        s_ *= 2
    incl1 = (lax.broadcasted_iota(jnp.int32, (CHUNK, CHUNK), 0)
             >= lax.broadcasted_iota(jnp.int32, (CHUNK, CHUNK), 1))
    gna = gna_ref[...]
    gnb = gnb_ref[...]
    zblk = jnp.zeros((CHUNK, LANES), BF16)

    def bdiag_packed(y):
        return jnp.concatenate([jnp.where(lo, y, 0), jnp.where(lo, 0, y)], axis=0)

    def bdiag_wide(y):
        return jnp.concatenate([jnp.concatenate([y[:, 0:LANES], zblk], axis=1),
                                jnp.concatenate([zblk, y[:, LANES:2 * LANES]], axis=1)], axis=0)

    def chunk_rows(c):
        return pl.ds(pl.multiple_of(c * CHUNK, CHUNK), CHUNK)

    chains = [(c, p) for c in range(nch) for p in range(npair)]
    crow = [slice(c * CHUNK, (c + 1) * CHUNK) for c in range(nch)]
    pcols = lambda p: slice(p * pair_w, (p + 1) * pair_w)
    kbds = [bdiag_wide(kn_ref[crow[c], pcols(p)]) for c, p in chains]
    kks = [_dot_nt(kn_ref[crow[c], pcols(p)], kbd) for (c, p), kbd in zip(chains, kbds)]
    qks = [_dot_nt(qn_ref[crow[c], pcols(p)], kbd) for (c, p), kbd in zip(chains, kbds)]
    bcs = [bcol_ref[crow[c], :] for c in range(nch)]
    bes = [betac_ref[crow[c], :] for c in range(nch)]
    brs = [brow_ref[c] for c in range(nch)]
    side()
    a_s, tinvs, pws = [], [], []
    for n_, (c, p) in enumerate(chains):
        h1, h2 = 2 * p, 2 * p + 1
        pick = lambda v, l0: jnp.where(lo, v[:, l0 + h1:l0 + h1 + 1], v[:, l0 + h2:l0 + h2 + 1])
        diff = pick(bcs[c], AIN_LANE) - jnp.where(lo[0:1], brs[c][h1:h1 + 1, :], brs[c][h2:h2 + 1, :])
        dm = jnp.where(incl, jnp.exp(jnp.where(incl, diff, 0.0)), 0.0)
        qkm_ref[c, p] = (qks[n_] * dm).astype(BF16)
        a = jnp.where(strict, pick(bes[c], BETA_LANE) * kks[n_] * dm, 0.0)
        dblk = jnp.where(blk[INV_BASE], a, 0.0)
        a_s.append(a)
        tinvs.append(eye - dblk)
        pws.append(dblk.astype(BF16))
    side()
    n = 2
    while n < INV_BASE:
        pws = [_dot(p_, bdiag_packed(p_)).astype(BF16) for p_ in pws]
        tinvs = [t_ + _dot(t_.astype(BF16), bdiag_packed(p_)) for t_, p_ in zip(tinvs, pws)]
        side()
        n *= 2
    while n < CHUNK:
        es = [jnp.where(blk[2 * n] & ~blk[n], a, 0.0).astype(BF16) for a in a_s]
        tbs = [t_.astype(BF16) for t_ in tinvs]
        tbds = [bdiag_packed(tb) for tb in tbs]
        tes = [_dot(tb, bdiag_packed(e)).astype(BF16) for tb, e in zip(tbs, es)]
        side()
        tinvs = [t_ - _dot(te, tbd) for t_, te, tbd in zip(tinvs, tes, tbds)]
        side()
        n *= 2
    for t_, (c, p) in zip(tinvs, chains):
        tb = t_.astype(BF16)
        uv_ref[crow[c], pcols(p)] = _dot(tb, bdiag_wide(bv_ref[crow[c], pcols(p)]))
        wk_ref[crow[c], pcols(p)] = _dot(tb, bdiag_wide(bek_ref[crow[c], pcols(p)])).astype(BF16)
    side(len(side_tasks))

    def chunk_body(c, carry):
        rows = chunk_rows(c)
        ha, hb_ = range(GLA_HEADS), range(GDN_HEADS)
        lk = lambda h: slice(h * dk_a, (h + 1) * dk_a)
        lv = lambda h: slice(h * dv_a, (h + 1) * dv_a)
        hc = lambda h: slice(h * GDN_DK, (h + 1) * GDN_DK)
        sa = [sgla_ref[0, h] for h in ha]
        sb = [sgdn_ref[0, h] for h in hb_]
        v_a = [va_ref[rows, lv(h)] for h in ha]
        dec_a = bdec_ref[c]
        dv_all = dvec_ref[c]
        sab = [s_.astype(BF16) for s_ in sa]
        sbb = [s_.astype(BF16) for s_ in sb]
        wq = [_dot(jnp.concatenate([wk_ref[rows, hc(h)], qdec_ref[rows, hc(h)]], axis=0), sbb[h]) for h in hb_]
        ws = [w_[0:CHUNK] for w_ in wq]
        qsb = [w_[CHUNK:2 * CHUNK] for w_ in wq]
        att = [jnp.where(incl1, _dot_nt(qe_ref[rows, lk(h)], ke_ref[rows, lk(h)]), 0.0).astype(BF16) for h in ha]
        oi = [_dot(qd_ref[rows, lk(h)], sab[h]) for h in ha]
        kv = [_dot_tn(kd_ref[rows, lk(h)], v_a[h]) for h in ha]
        u = [(uv_ref[rows, hc(h)] - ws[h]).astype(BF16) for h in hb_]
        qku = [_dot(qkm_ref[c, p], bdiag_wide(jnp.concatenate([u[2 * p], u[2 * p + 1]], axis=1)))
               for p in range(npair)]
        o_b = [qsb[h] + qku[h // 2][:, (h % 2) * GDN_DV:(h % 2 + 1) * GDN_DV] for h in hb_]
        sb_new = [dv_all[h:h + 1, :] * sb[h] + _dot_tn(kdec_ref[rows, hc(h)], u[h]) for h in hb_]
        o_a = [_dot(att[h], v_a[h]) + oi[h] for h in ha]
        sa_new = []
        for h in ha:
            dcol = jnp.broadcast_to(dec_a[:, lk(h)], (dk_a, dk_a)).T
            sa_new.append(jnp.concatenate([dcol] * (dv_a // dk_a), axis=1) * sa[h] + kv[h])
        for h in ha:
            o = o_a[h]
            o = o * lax.rsqrt(jnp.mean(o * o, axis=-1, keepdims=True) + EPS) * gna
            oa_ref[rows, lv(h)] = (o * sza_ref[rows, lv(h)]).astype(BF16)
            sgla_ref[0, h] = sa_new[h]
        for h in hb_:
            o = o_b[h]
            o = o * lax.rsqrt(jnp.mean(o * o, axis=-1, keepdims=True) + EPS) * gnb
            ob_ref[rows, hc(h)] = (o * szb_ref[rows, hc(h)]).astype(BF16)
            sgdn_ref[0, h] = sb_new[h]
        return carry

    lax.fori_loop(0, nch, chunk_body, 0, unroll=True)

    merged = (jax.nn.sigmoid(proj(o_ga, d)) * _dot(oa_ref[...], wpa_ref[...])
              + jax.nn.sigmoid(proj(o_gb, d)) * _dot(ob_ref[...], wpb_ref[...]))
    out = _dot(merged.astype(BF16), wout_ref[...])
    xn = x_ref[0] + mod_ref[0][:, 2 * d:3 * d] * out
    if final_norm:
        xn = xn * lax.rsqrt(jnp.mean(xn * xn, axis=-1, keepdims=True) + EPS) * gfin_ref[...]
    y_ref[0] = xn


def _resident(shape):
    zeros = (0,) * len(shape)
    return pl.BlockSpec(shape, lambda b, t: zeros, pipeline_mode=pl.Buffered(1))


def _layer(x, mod, consts, states, *, tc, n_valid, final_norm):
    bsz, t_len, d = x.shape
    nch = tc // CHUNK
    qk_a = d // 2
    has_state = states is not None
    const_specs = [_resident(c.shape) for c in consts]
    in_specs = [pl.BlockSpec((1, tc, d), lambda b, t: (b, t, 0)),
                pl.BlockSpec((1, 1, 3 * d), lambda b, t: (b, 0, 0))] + const_specs
    args = [x, mod.reshape(bsz, 1, 3 * d)] + list(consts)
    state_shapes = [(bsz, GLA_HEADS, qk_a // GLA_HEADS, d // GLA_HEADS),
                    (bsz, GDN_HEADS, GDN_DK, GDN_DV),
                    (bsz, CONV_W - 1, 3 * d)]
    state_specs = [pl.BlockSpec((1,) + s[1:], lambda b, t, n=len(s): (b,) + (0,) * (n - 1)) for s in state_shapes]
    if has_state:
        in_specs += state_specs
        args += list(states)
    scratch = [
        pltpu.VMEM((tc, d), BF16),
        pltpu.VMEM((tc, qk_a), BF16), pltpu.VMEM((tc, qk_a), BF16),
        pltpu.VMEM((tc, qk_a), BF16), pltpu.VMEM((tc, qk_a), BF16),
        pltpu.VMEM((tc, d), BF16),
        pltpu.VMEM((tc, d), F32),
        pltpu.VMEM((nch, 1, qk_a), F32),
        pltpu.VMEM((HIST + tc, 3 * d), F32),
        pltpu.VMEM((tc, d), BF16), pltpu.VMEM((tc, d), BF16),
        pltpu.VMEM((tc, d), BF16), pltpu.VMEM((tc, d), BF16),
        pltpu.VMEM((tc, d), BF16), pltpu.VMEM((tc, d), BF16),
        pltpu.VMEM((tc, d), F32),
        pltpu.VMEM((tc, LANES), F32), pltpu.VMEM((tc, LANES), F32),
        pltpu.VMEM((nch, GDN_HEADS, 2 * CHUNK), F32),
        pltpu.VMEM((nch, GDN_HEADS, LANES), F32),
        pltpu.VMEM((tc, d), F32), pltpu.VMEM((tc, d), BF16),
        pltpu.VMEM((nch, GDN_HEADS // 2, CHUNK, 2 * CHUNK), BF16),
        pltpu.VMEM((tc, qk_a), F32),
        pltpu.VMEM((tc, d), BF16), pltpu.VMEM((tc, d), BF16),
    ]
    kern = functools.partial(_layer_kernel, tc=tc, n_valid=n_valid, has_state=has_state,
                             final_norm=final_norm, d_model=d)
    return pl.pallas_call(
        kern,
        grid=(bsz, t_len // tc),
        in_specs=in_specs,
        out_specs=[pl.BlockSpec((1, tc, d), lambda b, t: (b, t, 0))] + state_specs,
        out_shape=[jax.ShapeDtypeStruct(x.shape, F32)] + [jax.ShapeDtypeStruct(s, F32) for s in state_shapes],
        scratch_shapes=scratch,
        compiler_params=pltpu.CompilerParams(dimension_semantics=("arbitrary", "arbitrary"),
                                             vmem_limit_bytes=VMEM_LIMIT_BYTES),
        name="gla_gdn_layer",
    )(*args)


def _permute_cast_kernel(w_ref, o_ref, *, segments, pad):
    w = w_ref[...]
    parts = [w[:, a:b] for a, b in segments] + [jnp.zeros((w.shape[0], pad), w.dtype)]
    o_ref[...] = jnp.concatenate(parts, axis=1).astype(o_ref.dtype)


def _permute_cast(w, segments, pad):
    rows, cols = w.shape
    out_cols = sum(b - a for a, b in segments) + pad
    bm = PREP_ROWS
    return pl.pallas_call(
        functools.partial(_permute_cast_kernel, segments=segments, pad=pad),
        grid=(rows // bm,),
        in_specs=[pl.BlockSpec((bm, cols), lambda i: (i, 0))],
        out_specs=pl.BlockSpec((bm, out_cols), lambda i: (i, 0)),
        out_shape=jax.ShapeDtypeStruct((rows, out_cols), BF16),
        name="permute_cast",
    )(w)


def _layer_consts(g_norm1, w_in, w_gk2, b_gk, w_conv, a_log, dt_bias, g_norm_a, g_norm_b,
                  w_pa, w_pb, w_out, g_final):
    d = w_in.shape[0]
    qk_a = d // 2
    o_gk = 2 * qk_a + 2 * d
    o_qkv = o_gk + GLA_RANK
    o_zb = o_qkv + 3 * d
    o_beta = o_zb + d
    o_a = o_beta + GDN_HEADS
    o_ga = o_a + GDN_HEADS
    pad = LANES - (GLA_RANK + 2 * GDN_HEADS)
    w_perm = _permute_cast(w_in, ((0, o_gk), (o_qkv, o_beta), (o_ga, o_ga + 2 * d), (o_gk, o_qkv), (o_beta, o_ga)), pad)
    w_at = w_in[:, o_a:o_ga].T.astype(BF16)
    w_gk = jnp.zeros((LANES, qk_a), F32).at[0:GLA_RANK].set(w_gk2).astype(BF16)
    lane_vec = lambda v: jnp.zeros((1, LANES), F32).at[0, AIN_LANE:AIN_LANE + GDN_HEADS].set(v)
    col_vec = lambda v: jnp.broadcast_to(v.reshape(GDN_HEADS, 1), (GDN_HEADS, CHUNK)).astype(F32)
    tu = jnp.arange(CHUNK)
    ut = jnp.tile(tu[:, None] <= tu[None, :], (1, 2)).astype(BF16)
    return [g_norm1.reshape(1, d), w_perm, w_at, w_gk, b_gk.reshape(1, qk_a), w_conv,
            lane_vec(a_log), lane_vec(dt_bias), col_vec(a_log), col_vec(dt_bias),
            g_norm_a.reshape(1, -1), g_norm_b.reshape(1, -1),
            w_pa.astype(BF16), w_pb.astype(BF16), w_out.astype(BF16), g_final.reshape(1, d), ut]


def _tile_consts(consts, tc):
    tt = jnp.arange(tc)
    bd = ((tt[:, None] // CHUNK == tt[None, :] // CHUNK) & (tt[None, :] <= tt[:, None])).astype(BF16)
    return consts[:-1] + [bd, consts[-1]]


PROMPT_TILE = 256


def kernel(x_prompt, x_sample, c_prompt, c_sample, state_gla, state_gdn, cache_conv_gdn, w_ada, b_ada, g_norm1, w_in, w_gk2, b_gk, w_conv, a_log, dt_bias, g_norm_a, g_norm_b, w_pa, w_pb, w_out, g_final):
    depth = w_in.shape[0]
    bp, tp, _ = x_prompt.shape
    bs, ts, _ = x_sample.shape
    assert tp % PROMPT_TILE == 0 and CONV_W - 1 <= ts <= CHUNK
    hp = x_prompt
    hs = jnp.pad(x_sample, ((0, 0), (0, CHUNK - ts), (0, 0)))
    outs_p, outs_s = [], []
    for layer in range(depth):
        last = layer == depth - 1
        mod = _adaln_mod(jnp.concatenate([c_prompt, c_sample], axis=0), w_ada[layer], b_ada[layer])
        lw = (g_norm1[layer], w_in[layer], w_gk2[layer], b_gk[layer], w_conv[layer], a_log[layer],
              dt_bias[layer], g_norm_a[layer], g_norm_b[layer], w_pa[layer], w_pb[layer], w_out[layer], g_final)
        consts = _layer_consts(*lw)
        hp, *st_p = _layer(hp, mod[:bp], _tile_consts(consts, PROMPT_TILE), None,
                           tc=PROMPT_TILE, n_valid=PROMPT_TILE, final_norm=last)
        st_in = (state_gla[layer], state_gdn[layer], cache_conv_gdn[layer])
        hs, *st_s = _layer(hs, mod[bp:], _tile_consts(consts, CHUNK), st_in,
                           tc=CHUNK, n_valid=ts, final_norm=last)
        outs_p.append(st_p)
        outs_s.append(st_s)

    stack = lambda outs, i: jnp.stack([o[i] for o in outs])
    return (hp, hs[:, :ts], stack(outs_p, 0), stack(outs_p, 1), stack(outs_p, 2),
            stack(outs_s, 0), stack(outs_s, 1), stack(outs_s, 2))
```

```python
import functools

import jax
import jax.numpy as jnp
from jax import lax
from jax.experimental import pallas as pl
from jax.experimental.pallas import tpu as pltpu

F32 = jnp.float32
BF16 = jnp.bfloat16

CHUNK = 64
EPS = 1e-6
GLA_HEADS = 4
GLA_RANK = 16
GLA_GATE_NORM = 16.0
GDN_HEADS = 8
GDN_DK = 128
GDN_DV = 128
CONV_W = 4
LANES = 128
HIST = 8
INV_BASE = 8
PREP_COLS = 256
BETA_LANE = GLA_RANK
AIN_LANE = GLA_RANK + GDN_HEADS
VMEM_LIMIT_BYTES = 56 * 1024 * 1024


def _dot(a, b):
    return jnp.dot(a, b, preferred_element_type=F32)


def _dot_nt(a, b):
    return lax.dot_general(a, b, (((1,), (1,)), ((), ())), preferred_element_type=F32)


def _dot_tn(a, b):
    return lax.dot_general(a, b, (((0,), (0,)), ((), ())), preferred_element_type=F32)


def _split2(x):
    hi = x.astype(BF16)
    return hi, (x - hi.astype(F32)).astype(BF16)


def _split3(x):
    hi = x.astype(BF16)
    r = x - hi.astype(F32)
    mid = r.astype(BF16)
    lo = (r - mid.astype(F32)).astype(BF16)
    return hi, mid, lo


def _softplus(x):
    return jnp.maximum(x, 0.0) + jnp.log1p(jnp.exp(-jnp.abs(x)))


def _log_sigmoid(x):
    return jnp.minimum(x, 0.0) - jnp.log1p(jnp.exp(-jnp.abs(x)))


def _silu(x):
    return x * jax.nn.sigmoid(x)


def _mod_kernel(c_ref, w_ref, b_ref, o_ref):
    s = _silu(c_ref[...]).astype(BF16)
    o_ref[...] = _dot(s, w_ref[...].astype(BF16)) + b_ref[...]


def _adaln_mod(c, w_ada, b_ada):
    n, d = c.shape
    d3 = w_ada.shape[1]
    bn = 512
    return pl.pallas_call(
        _mod_kernel,
        grid=(d3 // bn,),
        in_specs=[pl.BlockSpec((n, d), lambda j: (0, 0)),
                  pl.BlockSpec((d, bn), lambda j: (0, j)),
                  pl.BlockSpec((1, bn), lambda j: (0, j))],
        out_specs=pl.BlockSpec((n, bn), lambda j: (0, j)),
        out_shape=jax.ShapeDtypeStruct((n, d3), F32),
        name="adaln_mod",
    )(c, w_ada, b_ada.reshape(1, d3))


def _layer_kernel(*refs, tc, n_valid, has_state, final_norm, d_model):
    d = d_model
    qk_a = d // 2
    dk_a = qk_a // GLA_HEADS
    dv_a = d // GLA_HEADS
    nch = tc // CHUNK
    o_qa, o_ka, o_va, o_za = 0, qk_a, 2 * qk_a, 2 * qk_a + d
    o_qb = o_za + d
    o_kb, o_vb = o_qb + d, o_qb + 2 * d
    o_zb = o_qb + 3 * d
    o_ga, o_gb = o_zb + d, o_zb + 2 * d
    o_sm = o_gb + d

    it = iter(refs)
    x_ref, mod_ref, g1_ref, win_ref, wsm_ref, wat_ref, wgk_ref, bgk_ref, wconv_ref = (next(it) for _ in range(9))
    alane_ref, dlane_ref, acol_ref, dcol_ref = (next(it) for _ in range(4))
    gna_ref, gnb_ref, wpa_ref, wpb_ref, wout_ref, gfin_ref, bd_ref, ut_ref = (next(it) for _ in range(8))
    if has_state:
        sgla_in, sgdn_in, conv_in = (next(it) for _ in range(3))
    y_ref, sgla_ref, sgdn_ref, conv_ref = (next(it) for _ in range(4))
    (hb_ref, qe_ref, ke_ref, qd_ref, kd_ref, va_ref, sza_ref, bdec_ref, ubuf_ref,
     qn_ref, kn_ref, qdec_ref, kdec_ref, bv_ref, bek_ref, szb_ref,
     bcol_ref, betac_ref, brow_ref, dvec_ref, uv_ref, wk_ref, qkm_ref, ba_ref, oa_ref, ob_ref) = (next(it) for _ in range(26))

    t = pl.program_id(1)

    @pl.when(t == 0)
    def _init():
        if has_state:
            sgla_ref[...] = sgla_in[...]
            sgdn_ref[...] = sgdn_in[...]
            ubuf_ref[HIST - (CONV_W - 1):HIST, :] = conv_in[0]
        else:
            sgla_ref[...] = jnp.zeros_like(sgla_ref)
            sgdn_ref[...] = jnp.zeros_like(sgdn_ref)
            ubuf_ref[0:HIST, :] = jnp.zeros((HIST, ubuf_ref.shape[1]), F32)

    masked = n_valid < tc
    if masked:
        rowmask = lax.broadcasted_iota(jnp.int32, (tc, 1), 0) < n_valid

    def mrow(v):
        return jnp.where(rowmask, v, 0.0) if masked else v

    x = x_ref[0]
    mod = mod_ref[0]
    shift, scale = mod[:, 0:d], mod[:, d:2 * d]
    hn = x * lax.rsqrt(jnp.mean(x * x, axis=-1, keepdims=True) + EPS) * g1_ref[...]
    hb_ref[...] = (hn * (1.0 + scale) + shift).astype(BF16)

    def proj(c0, width):
        w = wsm_ref[...] if c0 == o_sm else win_ref[:, c0:c0 + width]
        return _dot(hb_ref[...], w)

    pair_w = 2 * GDN_DK
    half = d // 2

    def gdn_proj(jp, parts=(0, 1, 2)):
        for j in parts:
            c0 = j * d + jp * pair_w
            ubuf_ref[HIST:HIST + tc, c0:c0 + pair_w] = proj(o_qb + c0, pair_w)

    def va_task(i):
        va_ref[:, i * half:(i + 1) * half] = mrow(proj(o_va + i * half, half)).astype(BF16)

    def silu_task(dst_ref, c0, i):
        dst_ref[:, i * half:(i + 1) * half] = _silu(proj(c0 + i * half, half))

    ps = proj(o_sm, LANES)
    arows = [_dot_nt(wat_ref[...], hb_ref[c * CHUNK:(c + 1) * CHUNK, :]) for c in range(nch)]
    gdn_proj(0, (0,))
    lane = lax.broadcasted_iota(jnp.int32, (1, LANES), 1)
    betac_ref[...] = mrow(jax.nn.sigmoid(ps))
    g_col = -jnp.exp(alane_ref[...]) * _softplus(ps + dlane_ref[...])
    g_col = mrow(jnp.where((lane >= AIN_LANE) & (lane < AIN_LANE + GDN_HEADS), g_col, 0.0))
    g_rows = []
    for c in range(nch):
        g_row = -jnp.exp(acol_ref[...]) * _softplus(arows[c] + dcol_ref[...])
        if masked:
            colmask = (lax.broadcasted_iota(jnp.int32, (1, CHUNK), 1) + c * CHUNK) < n_valid
            g_row = jnp.where(colmask, g_row, 0.0)
        g_rows.append(g_row)
    gdn_proj(0, (1,))
    bd = bd_ref[...]
    ut = ut_ref[...]
    gh, gm, gl = _split3(g_col)
    b_col = _dot(bd, gh) + _dot(bd, gm) + _dot(bd, gl)
    b_rows = []
    for c in range(nch):
        rh, rm, rl = _split3(g_rows[c])
        b_rows.append(_dot(rh, ut) + _dot(rm, ut) + _dot(rl, ut))
    gdn_proj(0, (2,))
    va_task(0)
    bcol_ref[...] = b_col
    for c in range(nch):
        brow_ref[c] = b_rows[c]
        dvec_ref[c] = jnp.broadcast_to(jnp.exp(b_rows[c][:, CHUNK - 1:CHUNK]), (GDN_HEADS, LANES))
    e_b = jnp.exp(b_col)
    b_col3 = b_col.reshape(nch, CHUNK, LANES)
    e_lb = jnp.exp(b_col3[:, CHUNK - 1:CHUNK, :] - b_col3).reshape(tc, LANES)
    beta = betac_ref[...]
    va_task(1)

    def conv(c0, width):
        cols = slice(c0, c0 + width)
        full = ubuf_ref[:, cols]
        acc = full[HIST:] * wconv_ref[CONV_W - 1:CONV_W, cols]
        for i in range(CONV_W - 1):
            back = CONV_W - 1 - i
            acc = acc + pltpu.roll(full, back, 0)[HIST:] * wconv_ref[i:i + 1, cols]
        return _silu(acc)

    def gdn_prep(h):
        cols = slice(h * GDN_DK, (h + 1) * GDN_DK)
        qh, kh_, vh = (conv(j * d + h * GDN_DK, GDN_DK) for j in range(3))
        be_h = beta[:, BETA_LANE + h:BETA_LANE + h + 1]
        eb_h = e_b[:, AIN_LANE + h:AIN_LANE + h + 1]
        elb_h = e_lb[:, AIN_LANE + h:AIN_LANE + h + 1]
        qh = mrow(qh * lax.rsqrt(jnp.sum(qh * qh, axis=-1, keepdims=True) + EPS) * (GDN_DK ** -0.5))
        kh_ = mrow(kh_ * lax.rsqrt(jnp.sum(kh_ * kh_, axis=-1, keepdims=True) + EPS))
        vh = mrow(vh)
        qn_ref[:, cols] = qh.astype(BF16)
        kn_ref[:, cols] = kh_.astype(BF16)
        qdec_ref[:, cols] = (qh * eb_h).astype(BF16)
        kdec_ref[:, cols] = (kh_ * elb_h).astype(BF16)
        bv_ref[:, cols] = (be_h * vh).astype(BF16)
        bek_ref[:, cols] = ((be_h * eb_h) * kh_).astype(BF16)

    def gla_gate_task():
        gk = _log_sigmoid(_dot(ps.astype(BF16), wgk_ref[...]) + bgk_ref[...]) * (1.0 / GLA_GATE_NORM)
        gk = mrow(gk)
        kh, kl = _split2(gk)
        ba_ref[...] = _dot(bd, kh) + _dot(bd, kl)

    def gla_qk_task(which):
        b_a = ba_ref[...].reshape(nch, CHUNK, qk_a)
        b_mid = b_a[:, CHUNK // 2 - 1:CHUNK // 2, :]
        b_last = b_a[:, CHUNK - 1:CHUNK, :]
        if which == 0:
            bdec_ref[...] = jnp.exp(b_last)
            qa = mrow(proj(o_qa, qk_a) * (dk_a ** -0.5)).reshape(nch, CHUNK, qk_a)
            qe_ref[...] = (qa * jnp.exp(b_a - b_mid)).reshape(tc, qk_a).astype(BF16)
            qd_ref[...] = (qa * jnp.exp(b_a)).reshape(tc, qk_a).astype(BF16)
        else:
            ka = mrow(proj(o_ka, qk_a)).reshape(nch, CHUNK, qk_a)
            ke_ref[...] = (ka * jnp.exp(b_mid - b_a)).reshape(tc, qk_a).astype(BF16)
            kd_ref[...] = (ka * jnp.exp(b_last - b_a)).reshape(tc, qk_a).astype(BF16)

    side_tasks = [gla_gate_task,
                  functools.partial(gla_qk_task, 0), functools.partial(gla_qk_task, 1),
                  functools.partial(silu_task, sza_ref, o_za, 0), functools.partial(silu_task, sza_ref, o_za, 1),
                  functools.partial(silu_task, szb_ref, o_zb, 0), functools.partial(silu_task, szb_ref, o_zb, 1)]

    def side(n=1):
        for _ in range(n):
            if side_tasks:
                side_tasks.pop(0)()

    n_pairs = GDN_HEADS // 2
    for jp in range(n_pairs):
        more = jp + 1 < n_pairs
        if more:
            gdn_proj(jp + 1, (0,))
        gdn_prep(2 * jp)
        if more:
            gdn_proj(jp + 1, (1,))
        side()
        gdn_prep(2 * jp + 1)
        if more:
            gdn_proj(jp + 1, (2,))
        side()

    tail = ubuf_ref[HIST + n_valid - (CONV_W - 1):HIST + n_valid, :]
    ubuf_ref[HIST - (CONV_W - 1):HIST, :] = tail
    conv_ref[0] = tail

    npair = GDN_HEADS // 2
    ri = lax.broadcasted_iota(jnp.int32, (CHUNK, 2 * CHUNK), 0)
    li = lax.broadcasted_iota(jnp.int32, (CHUNK, 2 * CHUNK), 1)
    ci = li & (CHUNK - 1)
    lo = li < CHUNK
    incl = ri >= ci
    strict = ri > ci
    eye = jnp.where(ri == ci, 1.0, 0.0).astype(F32)
    blk = {}
    s_ = INV_BASE
    while s_ <= CHUNK:
        sh = s_.bit_length() - 1
        blk[s_] = (ri >> sh) == (ci >> sh)
        s_ *= 2
    incl1 = (lax.broadcasted_iota(jnp.int32, (CHUNK, CHUNK), 0)
             >= lax.broadcasted_iota(jnp.int32, (CHUNK, CHUNK), 1))
    gna = gna_ref[...]
    gnb = gnb_ref[...]
    zblk = jnp.zeros((CHUNK, LANES), BF16)

    def bdiag_packed(y):
        return jnp.concatenate([jnp.where(lo, y, 0), jnp.where(lo, 0, y)], axis=0)

    def bdiag_wide(y):
        return jnp.concatenate([jnp.concatenate([y[:, 0:LANES], zblk], axis=1),
                                jnp.concatenate([zblk, y[:, LANES:2 * LANES]], axis=1)], axis=0)

    def chunk_rows(c):
        return pl.ds(pl.multiple_of(c * CHUNK, CHUNK), CHUNK)

    chains = [(c, p) for c in range(nch) for p in range(npair)]
    crow = [slice(c * CHUNK, (c + 1) * CHUNK) for c in range(nch)]
    pcols = lambda p: slice(p * pair_w, (p + 1) * pair_w)
    kbds = [bdiag_wide(kn_ref[crow[c], pcols(p)]) for c, p in chains]
    kks = [_dot_nt(kn_ref[crow[c], pcols(p)], kbd) for (c, p), kbd in zip(chains, kbds)]
    qks = [_dot_nt(qn_ref[crow[c], pcols(p)], kbd) for (c, p), kbd in zip(chains, kbds)]
    bcs = [bcol_ref[crow[c], :] for c in range(nch)]
    bes = [betac_ref[crow[c], :] for c in range(nch)]
    brs = [brow_ref[c] for c in range(nch)]
    side()
    a_s, tinvs, pws = [], [], []
    for n_, (c, p) in enumerate(chains):
        h1, h2 = 2 * p, 2 * p + 1
        pick = lambda v, l0: jnp.where(lo, v[:, l0 + h1:l0 + h1 + 1], v[:, l0 + h2:l0 + h2 + 1])
        diff = pick(bcs[c], AIN_LANE) - jnp.where(lo[0:1], brs[c][h1:h1 + 1, :], brs[c][h2:h2 + 1, :])
        dm = jnp.where(incl, jnp.exp(jnp.where(incl, diff, 0.0)), 0.0)
        qkm_ref[c, p] = (qks[n_] * dm).astype(BF16)
        a = jnp.where(strict, pick(bes[c], BETA_LANE) * kks[n_] * dm, 0.0)
        dblk = jnp.where(blk[INV_BASE], a, 0.0)
        a_s.append(a)
        tinvs.append(eye - dblk)
        pws.append(dblk.astype(BF16))
    side()
    n = 2
    while n < INV_BASE:
        pws = [_dot(p_, bdiag_packed(p_)).astype(BF16) for p_ in pws]
        tinvs = [t_ + _dot(t_.astype(BF16), bdiag_packed(p_)) for t_, p_ in zip(tinvs, pws)]
        side()
        n *= 2
    while n < CHUNK:
        es = [jnp.where(blk[2 * n] & ~blk[n], a, 0.0).astype(BF16) for a in a_s]
        tbs = [t_.astype(BF16) for t_ in tinvs]
        tbds = [bdiag_packed(tb) for tb in tbs]
        tes = [_dot(tb, bdiag_packed(e)).astype(BF16) for tb, e in zip(tbs, es)]
        side()
        tinvs = [t_ - _dot(te, tbd) for t_, te, tbd in zip(tinvs, tes, tbds)]
        side()
        n *= 2
    for t_, (c, p) in zip(tinvs, chains):
        tb = t_.astype(BF16)
        uv_ref[crow[c], pcols(p)] = _dot(tb, bdiag_wide(bv_ref[crow[c], pcols(p)]))
        wk_ref[crow[c], pcols(p)] = _dot(tb, bdiag_wide(bek_ref[crow[c], pcols(p)])).astype(BF16)
    side(len(side_tasks))

    def chunk_body(c, carry):
        rows = chunk_rows(c)
        ha, hb_ = range(GLA_HEADS), range(GDN_HEADS)
        lk = lambda h: slice(h * dk_a, (h + 1) * dk_a)
        lv = lambda h: slice(h * dv_a, (h + 1) * dv_a)
        hc = lambda h: slice(h * GDN_DK, (h + 1) * GDN_DK)
        sa = [sgla_ref[0, h] for h in ha]
        sb = [sgdn_ref[0, h] for h in hb_]
        v_a = [va_ref[rows, lv(h)] for h in ha]
        dec_a = bdec_ref[c]
        dv_all = dvec_ref[c]
        sab = [s_.astype(BF16) for s_ in sa]
        sbb = [s_.astype(BF16) for s_ in sb]
        wq = [_dot(jnp.concatenate([wk_ref[rows, hc(h)], qdec_ref[rows, hc(h)]], axis=0), sbb[h]) for h in hb_]
        ws = [w_[0:CHUNK] for w_ in wq]
        qsb = [w_[CHUNK:2 * CHUNK] for w_ in wq]
        att = [jnp.where(incl1, _dot_nt(qe_ref[rows, lk(h)], ke_ref[rows, lk(h)]), 0.0).astype(BF16) for h in ha]
        oi = [_dot(qd_ref[rows, lk(h)], sab[h]) for h in ha]
        kv = [_dot_tn(kd_ref[rows, lk(h)], v_a[h]) for h in ha]
        u = [(uv_ref[rows, hc(h)] - ws[h]).astype(BF16) for h in hb_]
        qku = [_dot(qkm_ref[c, p], bdiag_wide(jnp.concatenate([u[2 * p], u[2 * p + 1]], axis=1)))
               for p in range(npair)]
        o_b = [qsb[h] + qku[h // 2][:, (h % 2) * GDN_DV:(h % 2 + 1) * GDN_DV] for h in hb_]
        sb_new = [dv_all[h:h + 1, :] * sb[h] + _dot_tn(kdec_ref[rows, hc(h)], u[h]) for h in hb_]
        o_a = [_dot(att[h], v_a[h]) + oi[h] for h in ha]
        sa_new = []
        for h in ha:
            dcol = jnp.broadcast_to(dec_a[:, lk(h)], (dk_a, dk_a)).T
            sa_new.append(jnp.concatenate([dcol] * (dv_a // dk_a), axis=1) * sa[h] + kv[h])
        for h in ha:
            o = o_a[h]
            o = o * lax.rsqrt(jnp.mean(o * o, axis=-1, keepdims=True) + EPS) * gna
            oa_ref[rows, lv(h)] = (o * sza_ref[rows, lv(h)]).astype(BF16)
            sgla_ref[0, h] = sa_new[h]
        for h in hb_:
            o = o_b[h]
            o = o * lax.rsqrt(jnp.mean(o * o, axis=-1, keepdims=True) + EPS) * gnb
            ob_ref[rows, hc(h)] = (o * szb_ref[rows, hc(h)]).astype(BF16)
            sgdn_ref[0, h] = sb_new[h]
        return carry

    lax.fori_loop(0, nch, chunk_body, 0, unroll=True)

    merged = (jax.nn.sigmoid(proj(o_ga, d)) * _dot(oa_ref[...], wpa_ref[...])
              + jax.nn.sigmoid(proj(o_gb, d)) * _dot(ob_ref[...], wpb_ref[...]))
    out = _dot(merged.astype(BF16), wout_ref[...])
    xn = x_ref[0] + mod_ref[0][:, 2 * d:3 * d] * out
    if final_norm:
        xn = xn * lax.rsqrt(jnp.mean(xn * xn, axis=-1, keepdims=True) + EPS) * gfin_ref[...]
    y_ref[0] = xn


def _resident(shape):
    zeros = (0,) * len(shape)
    return pl.BlockSpec(shape, lambda b, t: zeros, pipeline_mode=pl.Buffered(1))


def _layer(x, mod, consts, states, *, tc, n_valid, final_norm):
    bsz, t_len, d = x.shape
    nch = tc // CHUNK
    qk_a = d // 2
    has_state = states is not None
    const_specs = [_resident(c.shape) for c in consts]
    in_specs = [pl.BlockSpec((1, tc, d), lambda b, t: (b, t, 0)),
                pl.BlockSpec((1, 1, 3 * d), lambda b, t: (b, 0, 0))] + const_specs
    args = [x, mod.reshape(bsz, 1, 3 * d)] + list(consts)
    state_shapes = [(bsz, GLA_HEADS, qk_a // GLA_HEADS, d // GLA_HEADS),
                    (bsz, GDN_HEADS, GDN_DK, GDN_DV),
                    (bsz, CONV_W - 1, 3 * d)]
    state_specs = [pl.BlockSpec((1,) + s[1:], lambda b, t, n=len(s): (b,) + (0,) * (n - 1)) for s in state_shapes]
    if has_state:
        in_specs += state_specs
        args += list(states)
    scratch = [
        pltpu.VMEM((tc, d), BF16),
        pltpu.VMEM((tc, qk_a), BF16), pltpu.VMEM((tc, qk_a), BF16),
        pltpu.VMEM((tc, qk_a), BF16), pltpu.VMEM((tc, qk_a), BF16),
        pltpu.VMEM((tc, d), BF16),
        pltpu.VMEM((tc, d), F32),
        pltpu.VMEM((nch, 1, qk_a), F32),
        pltpu.VMEM((HIST + tc, 3 * d), F32),
        pltpu.VMEM((tc, d), BF16), pltpu.VMEM((tc, d), BF16),
        pltpu.VMEM((tc, d), BF16), pltpu.VMEM((tc, d), BF16),
        pltpu.VMEM((tc, d), BF16), pltpu.VMEM((tc, d), BF16),
        pltpu.VMEM((tc, d), F32),
        pltpu.VMEM((tc, LANES), F32), pltpu.VMEM((tc, LANES), F32),
        pltpu.VMEM((nch, GDN_HEADS, 2 * CHUNK), F32),
        pltpu.VMEM((nch, GDN_HEADS, LANES), F32),
        pltpu.VMEM((tc, d), F32), pltpu.VMEM((tc, d), BF16),
        pltpu.VMEM((nch, GDN_HEADS // 2, CHUNK, 2 * CHUNK), BF16),
        pltpu.VMEM((tc, qk_a), F32),
        pltpu.VMEM((tc, d), BF16), pltpu.VMEM((tc, d), BF16),
    ]
    kern = functools.partial(_layer_kernel, tc=tc, n_valid=n_valid, has_state=has_state,
                             final_norm=final_norm, d_model=d)
    return pl.pallas_call(
        kern,
        grid=(bsz, t_len // tc),
        in_specs=in_specs,
        out_specs=[pl.BlockSpec((1, tc, d), lambda b, t: (b, t, 0))] + state_specs,
        out_shape=[jax.ShapeDtypeStruct(x.shape, F32)] + [jax.ShapeDtypeStruct(s, F32) for s in state_shapes],
        scratch_shapes=scratch,
        compiler_params=pltpu.CompilerParams(dimension_semantics=("arbitrary", "arbitrary"),
                                             vmem_limit_bytes=VMEM_LIMIT_BYTES),
        name="gla_gdn_layer",
    )(*args)


def _transpose_cast_kernel(wt_ref, o_ref):
    o_ref[...] = wt_ref[...].T.astype(o_ref.dtype)


def _transpose_cast(wt, segments):
    _, rows = wt.shape
    starts, dst = [], 0
    for a, b in segments:
        assert (b - a) % PREP_COLS == 0
        starts.append((dst // PREP_COLS, a))
        dst += b - a

    def src_col(j):
        col = j * 0
        for blk0, a in starts:
            col = jnp.where(j >= blk0, a + (j - blk0) * PREP_COLS, col)
        return pl.multiple_of(col, 8)

    return pl.pallas_call(
        _transpose_cast_kernel,
        grid=(dst // PREP_COLS,),
        in_specs=[pl.BlockSpec((pl.Element(PREP_COLS), pl.Element(rows)), lambda j: (src_col(j), 0))],
        out_specs=pl.BlockSpec((rows, PREP_COLS), lambda j: (0, j)),
        out_shape=jax.ShapeDtypeStruct((rows, dst), BF16),
        name="transpose_cast",
    )(wt)


def _layer_consts(g_norm1, w_in, w_gk2, b_gk, w_conv, a_log, dt_bias, g_norm_a, g_norm_b,
                  w_pa, w_pb, w_out, g_final):
    d = w_in.shape[0]
    qk_a = d // 2
    o_gk = 2 * qk_a + 2 * d
    o_qkv = o_gk + GLA_RANK
    o_zb = o_qkv + 3 * d
    o_beta = o_zb + d
    o_a = o_beta + GDN_HEADS
    o_ga = o_a + GDN_HEADS
    pad = LANES - (GLA_RANK + 2 * GDN_HEADS)
    w_t = w_in.T
    w_perm = _transpose_cast(w_t, ((0, o_gk), (o_qkv, o_beta), (o_ga, o_ga + 2 * d)))
    w_sm = jnp.concatenate([w_t[o_gk:o_qkv], w_t[o_beta:o_ga], jnp.zeros((pad, d), w_in.dtype)], axis=0).T.astype(BF16)
    w_at = w_t[o_a:o_ga].astype(BF16)
    w_gk = jnp.zeros((LANES, qk_a), F32).at[0:GLA_RANK].set(w_gk2).astype(BF16)
    lane_vec = lambda v: jnp.zeros((1, LANES), F32).at[0, AIN_LANE:AIN_LANE + GDN_HEADS].set(v)
    col_vec = lambda v: jnp.broadcast_to(v.reshape(GDN_HEADS, 1), (GDN_HEADS, CHUNK)).astype(F32)
    tu = jnp.arange(CHUNK)
    ut = jnp.tile(tu[:, None] <= tu[None, :], (1, 2)).astype(BF16)
    return [g_norm1.reshape(1, d), w_perm, w_sm, w_at, w_gk, b_gk.reshape(1, qk_a), w_conv,
            lane_vec(a_log), lane_vec(dt_bias), col_vec(a_log), col_vec(dt_bias),
            g_norm_a.reshape(1, -1), g_norm_b.reshape(1, -1),
            w_pa.astype(BF16), w_pb.astype(BF16), w_out.astype(BF16), g_final.reshape(1, d), ut]


def _tile_consts(consts, tc):
    tt = jnp.arange(tc)
    bd = ((tt[:, None] // CHUNK == tt[None, :] // CHUNK) & (tt[None, :] <= tt[:, None])).astype(BF16)
    return consts[:-1] + [bd, consts[-1]]


PROMPT_TILE = 256


def kernel(x_prompt, x_sample, c_prompt, c_sample, state_gla, state_gdn, cache_conv_gdn, w_ada, b_ada, g_norm1, w_in, w_gk2, b_gk, w_conv, a_log, dt_bias, g_norm_a, g_norm_b, w_pa, w_pb, w_out, g_final):
    depth = w_in.shape[0]
    bp, tp, _ = x_prompt.shape
    bs, ts, _ = x_sample.shape
    assert tp % PROMPT_TILE == 0 and CONV_W - 1 <= ts <= CHUNK
    hp = x_prompt
    hs = jnp.pad(x_sample, ((0, 0), (0, CHUNK - ts), (0, 0)))
    outs_p, outs_s = [], []
    for layer in range(depth):
        last = layer == depth - 1
        mod = _adaln_mod(jnp.concatenate([c_prompt, c_sample], axis=0), w_ada[layer], b_ada[layer])
        lw = (g_norm1[layer], w_in[layer], w_gk2[layer], b_gk[layer], w_conv[layer], a_log[layer],
              dt_bias[layer], g_norm_a[layer], g_norm_b[layer], w_pa[layer], w_pb[layer], w_out[layer], g_final)
        consts = _layer_consts(*lw)
        hp, *st_p = _layer(hp, mod[:bp], _tile_consts(consts, PROMPT_TILE), None,
                           tc=PROMPT_TILE, n_valid=PROMPT_TILE, final_norm=last)
        st_in = (state_gla[layer], state_gdn[layer], cache_conv_gdn[layer])
        hs, *st_s = _layer(hs, mod[bp:], _tile_consts(consts, CHUNK), st_in,
                           tc=CHUNK, n_valid=ts, final_norm=last)
        outs_p.append(st_p)
        outs_s.append(st_s)

    stack = lambda outs, i: jnp.stack([o[i] for o in outs])
    return (hp, hs[:, :ts], stack(outs_p, 0), stack(outs_p, 1), stack(outs_p, 2),
            stack(outs_s, 0), stack(outs_s, 1), stack(outs_s, 2))
```

```python
import functools

import jax
import jax.numpy as jnp
from jax import lax
from jax.experimental import pallas as pl
from jax.experimental.pallas import tpu as pltpu

F32 = jnp.float32
BF16 = jnp.bfloat16

CHUNK = 64
EPS = 1e-6
GLA_HEADS = 4
GLA_RANK = 16
GLA_GATE_NORM = 16.0
GDN_HEADS = 8
GDN_DK = 128
GDN_DV = 128
CONV_W = 4
LANES = 128
HIST = 8
INV_BASE = 8
PREP_ROWS = 128
BETA_LANE = GLA_RANK
AIN_LANE = GLA_RANK + GDN_HEADS
VMEM_LIMIT_BYTES = 60 * 1024 * 1024


def _dot(a, b):
    return jnp.dot(a, b, preferred_element_type=F32)


def _dot_nt(a, b):
    return lax.dot_general(a, b, (((1,), (1,)), ((), ())), preferred_element_type=F32)


def _dot_tn(a, b):
    return lax.dot_general(a, b, (((0,), (0,)), ((), ())), preferred_element_type=F32)


def _split2(x):
    hi = x.astype(BF16)
    return hi, (x - hi.astype(F32)).astype(BF16)


def _split3(x):
    hi = x.astype(BF16)
    r = x - hi.astype(F32)
    mid = r.astype(BF16)
    lo = (r - mid.astype(F32)).astype(BF16)
    return hi, mid, lo


def _softplus(x):
    return jnp.maximum(x, 0.0) + jnp.log1p(jnp.exp(-jnp.abs(x)))


def _log_sigmoid(x):
    return jnp.minimum(x, 0.0) - jnp.log1p(jnp.exp(-jnp.abs(x)))


def _silu(x):
    return x * jax.nn.sigmoid(x)


def _mod_kernel(c_ref, w_ref, b_ref, o_ref):
    s = _silu(c_ref[...]).astype(BF16)
    o_ref[...] = _dot(s, w_ref[...].astype(BF16)) + b_ref[...]


def _adaln_mod(c, w_ada, b_ada):
    n, d = c.shape
    d3 = w_ada.shape[1]
    bn = 512
    return pl.pallas_call(
        _mod_kernel,
        grid=(d3 // bn,),
        in_specs=[pl.BlockSpec((n, d), lambda j: (0, 0)),
                  pl.BlockSpec((d, bn), lambda j: (0, j)),
                  pl.BlockSpec((1, bn), lambda j: (0, j))],
        out_specs=pl.BlockSpec((n, bn), lambda j: (0, j)),
        out_shape=jax.ShapeDtypeStruct((n, d3), F32),
        name="adaln_mod",
    )(c, w_ada, b_ada.reshape(1, d3))


def _layer_kernel(*refs, sub, **static):
    for s in range(sub):
        _tile_body(refs, s, **static)


def _tile_body(refs, s, *, tc, n_valid, has_state, final_norm, d_model):
    d = d_model
    qk_a = d // 2
    dk_a = qk_a // GLA_HEADS
    dv_a = d // GLA_HEADS
    nch = tc // CHUNK
    o_qa, o_ka, o_va, o_za = 0, qk_a, 2 * qk_a, 2 * qk_a + d
    o_qb = o_za + d
    o_kb, o_vb = o_qb + d, o_qb + 2 * d
    o_zb = o_qb + 3 * d
    o_ga, o_gb = o_zb + d, o_zb + 2 * d
    o_sm = o_gb + d

    it = iter(refs)
    x_ref, mod_ref, g1_ref, win_ref, wat_ref, wgk_ref, bgk_ref, wconv_ref = (next(it) for _ in range(8))
    alane_ref, dlane_ref, acol_ref, dcol_ref = (next(it) for _ in range(4))
    gna_ref, gnb_ref, wpa_ref, wpb_ref, wout_ref, gfin_ref, bd_ref, ut_ref = (next(it) for _ in range(8))
    if has_state:
        sgla_in, sgdn_in, conv_in = (next(it) for _ in range(3))
    y_ref, sgla_ref, sgdn_ref, conv_ref = (next(it) for _ in range(4))
    (hb_ref, qe_ref, ke_ref, qd_ref, kd_ref, va_ref, sza_ref, bdec_ref, ubuf_ref,
     qn_ref, kn_ref, qdec_ref, kdec_ref, bv_ref, bek_ref, szb_ref,
     bcol_ref, betac_ref, brow_ref, dvec_ref, uv_ref, wk_ref, qkm_ref, ba_ref, oa_ref, ob_ref) = (next(it) for _ in range(26))

    x_t = x_ref.at[0, pl.ds(s * tc, tc)]
    y_t = y_ref.at[0, pl.ds(s * tc, tc)]

    def _init():
        if has_state:
            sgla_ref[...] = sgla_in[...]
            sgdn_ref[...] = sgdn_in[...]
            ubuf_ref[HIST - (CONV_W - 1):HIST, :] = conv_in[0]
        else:
            sgla_ref[...] = jnp.zeros_like(sgla_ref)
            sgdn_ref[...] = jnp.zeros_like(sgdn_ref)
            ubuf_ref[0:HIST, :] = jnp.zeros((HIST, ubuf_ref.shape[1]), F32)

    if s == 0:
        pl.when(pl.program_id(1) == 0)(_init)

    masked = n_valid < tc
    if masked:
        rowmask = lax.broadcasted_iota(jnp.int32, (tc, 1), 0) < n_valid

    def mrow(v):
        return jnp.where(rowmask, v, 0.0) if masked else v

    x = x_t[...]
    mod = mod_ref[0]
    shift, scale = mod[:, 0:d], mod[:, d:2 * d]
    hn = x * lax.rsqrt(jnp.mean(x * x, axis=-1, keepdims=True) + EPS) * g1_ref[...]
    hb_ref[...] = (hn * (1.0 + scale) + shift).astype(BF16)

    def proj(c0, width):
        return _dot(hb_ref[...], win_ref[:, c0:c0 + width])

    pair_w = 2 * GDN_DK
    half = d // 2

    def gdn_proj(jp, parts=(0, 1, 2)):
        for j in parts:
            c0 = j * d + jp * pair_w
            ubuf_ref[HIST:HIST + tc, c0:c0 + pair_w] = proj(o_qb + c0, pair_w)

    def va_task(i):
        va_ref[:, i * half:(i + 1) * half] = mrow(proj(o_va + i * half, half)).astype(BF16)

    def silu_task(dst_ref, c0, i):
        dst_ref[:, i * half:(i + 1) * half] = _silu(proj(c0 + i * half, half))

    ps = proj(o_sm, LANES)
    arows = [_dot_nt(wat_ref[...], hb_ref[c * CHUNK:(c + 1) * CHUNK, :]) for c in range(nch)]
    gdn_proj(0, (0,))
    lane = lax.broadcasted_iota(jnp.int32, (1, LANES), 1)
    betac_ref[...] = mrow(jax.nn.sigmoid(ps))
    g_col = -jnp.exp(alane_ref[...]) * _softplus(ps + dlane_ref[...])
    g_col = mrow(jnp.where((lane >= AIN_LANE) & (lane < AIN_LANE + GDN_HEADS), g_col, 0.0))
    g_rows = []
    for c in range(nch):
        g_row = -jnp.exp(acol_ref[...]) * _softplus(arows[c] + dcol_ref[...])
        if masked:
            colmask = (lax.broadcasted_iota(jnp.int32, (1, CHUNK), 1) + c * CHUNK) < n_valid
            g_row = jnp.where(colmask, g_row, 0.0)
        g_rows.append(g_row)
    gdn_proj(0, (1,))
    bd = bd_ref[...]
    ut = ut_ref[...]
    gh, gm, gl = _split3(g_col)
    b_col = _dot(bd, gh) + _dot(bd, gm) + _dot(bd, gl)
    b_rows = []
    for c in range(nch):
        rh, rm, rl = _split3(g_rows[c])
        b_rows.append(_dot(rh, ut) + _dot(rm, ut) + _dot(rl, ut))
    gdn_proj(0, (2,))
    va_task(0)
    bcol_ref[...] = b_col
    for c in range(nch):
        brow_ref[c] = b_rows[c]
        dvec_ref[c] = jnp.broadcast_to(jnp.exp(b_rows[c][:, CHUNK - 1:CHUNK]), (GDN_HEADS, LANES))
    e_b = jnp.exp(b_col)
    b_col3 = b_col.reshape(nch, CHUNK, LANES)
    e_lb = jnp.exp(b_col3[:, CHUNK - 1:CHUNK, :] - b_col3).reshape(tc, LANES)
    beta = betac_ref[...]
    va_task(1)

    def conv(c0, width):
        cols = slice(c0, c0 + width)
        full = ubuf_ref[:, cols]
        acc = full[HIST:] * wconv_ref[CONV_W - 1:CONV_W, cols]
        for i in range(CONV_W - 1):
            back = CONV_W - 1 - i
            acc = acc + pltpu.roll(full, back, 0)[HIST:] * wconv_ref[i:i + 1, cols]
        return _silu(acc)

    def gdn_prep(h):
        cols = slice(h * GDN_DK, (h + 1) * GDN_DK)
        qh, kh_, vh = (conv(j * d + h * GDN_DK, GDN_DK) for j in range(3))
        be_h = beta[:, BETA_LANE + h:BETA_LANE + h + 1]
        eb_h = e_b[:, AIN_LANE + h:AIN_LANE + h + 1]
        elb_h = e_lb[:, AIN_LANE + h:AIN_LANE + h + 1]
        qh = mrow(qh * lax.rsqrt(jnp.sum(qh * qh, axis=-1, keepdims=True) + EPS) * (GDN_DK ** -0.5))
        kh_ = mrow(kh_ * lax.rsqrt(jnp.sum(kh_ * kh_, axis=-1, keepdims=True) + EPS))
        vh = mrow(vh)
        qn_ref[:, cols] = qh.astype(BF16)
        kn_ref[:, cols] = kh_.astype(BF16)
        qdec_ref[:, cols] = (qh * eb_h).astype(BF16)
        kdec_ref[:, cols] = (kh_ * elb_h).astype(BF16)
        bv_ref[:, cols] = (be_h * vh).astype(BF16)
        bek_ref[:, cols] = ((be_h * eb_h) * kh_).astype(BF16)

    def gla_gate_task():
        gk = _log_sigmoid(_dot(ps.astype(BF16), wgk_ref[...]) + bgk_ref[...]) * (1.0 / GLA_GATE_NORM)
        gk = mrow(gk)
        kh, kl = _split2(gk)
        ba_ref[...] = _dot(bd, kh) + _dot(bd, kl)

    def gla_qk_task(which):
        b_a = ba_ref[...].reshape(nch, CHUNK, qk_a)
        b_mid = b_a[:, CHUNK // 2 - 1:CHUNK // 2, :]
        b_last = b_a[:, CHUNK - 1:CHUNK, :]
        if which == 0:
            bdec_ref[...] = jnp.exp(b_last)
            qa = mrow(proj(o_qa, qk_a) * (dk_a ** -0.5)).reshape(nch, CHUNK, qk_a)
            qe_ref[...] = (qa * jnp.exp(b_a - b_mid)).reshape(tc, qk_a).astype(BF16)
            qd_ref[...] = (qa * jnp.exp(b_a)).reshape(tc, qk_a).astype(BF16)
        else:
            ka = mrow(proj(o_ka, qk_a)).reshape(nch, CHUNK, qk_a)
            ke_ref[...] = (ka * jnp.exp(b_mid - b_a)).reshape(tc, qk_a).astype(BF16)
            kd_ref[...] = (ka * jnp.exp(b_last - b_a)).reshape(tc, qk_a).astype(BF16)

    side_tasks = [gla_gate_task,
                  functools.partial(gla_qk_task, 0), functools.partial(gla_qk_task, 1),
                  functools.partial(silu_task, sza_ref, o_za, 0), functools.partial(silu_task, sza_ref, o_za, 1),
                  functools.partial(silu_task, szb_ref, o_zb, 0), functools.partial(silu_task, szb_ref, o_zb, 1)]

    def side(n=1):
        for _ in range(n):
            if side_tasks:
                side_tasks.pop(0)()

    n_pairs = GDN_HEADS // 2
    for jp in range(n_pairs):
        more = jp + 1 < n_pairs
        if more:
            gdn_proj(jp + 1, (0,))
        gdn_prep(2 * jp)
        if more:
            gdn_proj(jp + 1, (1,))
        side()
        gdn_prep(2 * jp + 1)
        if more:
            gdn_proj(jp + 1, (2,))
        side()

    tail = ubuf_ref[HIST + n_valid - (CONV_W - 1):HIST + n_valid, :]
    ubuf_ref[HIST - (CONV_W - 1):HIST, :] = tail
    conv_ref[0] = tail

    npair = GDN_HEADS // 2
    ri = lax.broadcasted_iota(jnp.int32, (CHUNK, 2 * CHUNK), 0)
    li = lax.broadcasted_iota(jnp.int32, (CHUNK, 2 * CHUNK), 1)
    ci = li & (CHUNK - 1)
    lo = li < CHUNK
    incl = ri >= ci
    strict = ri > ci
    eye = jnp.where(ri == ci, 1.0, 0.0).astype(F32)
    blk = {}
    s_ = INV_BASE
    while s_ <= CHUNK:
        sh = s_.bit_length() - 1
        blk[s_] = (ri >> sh) == (ci >> sh)
        s_ *= 2
    incl1 = (lax.broadcasted_iota(jnp.int32, (CHUNK, CHUNK), 0)
             >= lax.broadcasted_iota(jnp.int32, (CHUNK, CHUNK), 1))
    gna = gna_ref[...]
    gnb = gnb_ref[...]
    zblk = jnp.zeros((CHUNK, LANES), BF16)

    def bdiag_packed(y):
        return jnp.concatenate([jnp.where(lo, y, 0), jnp.where(lo, 0, y)], axis=0)

    def bdiag_wide(y):
        return jnp.concatenate([jnp.concatenate([y[:, 0:LANES], zblk], axis=1),
                                jnp.concatenate([zblk, y[:, LANES:2 * LANES]], axis=1)], axis=0)

    def chunk_rows(c):
        return pl.ds(pl.multiple_of(c * CHUNK, CHUNK), CHUNK)

    chains = [(c, p) for c in range(nch) for p in range(npair)]
    crow = [slice(c * CHUNK, (c + 1) * CHUNK) for c in range(nch)]
    pcols = lambda p: slice(p * pair_w, (p + 1) * pair_w)
    kbds = [bdiag_wide(kn_ref[crow[c], pcols(p)]) for c, p in chains]
    kks = [_dot_nt(kn_ref[crow[c], pcols(p)], kbd) for (c, p), kbd in zip(chains, kbds)]
    qks = [_dot_nt(qn_ref[crow[c], pcols(p)], kbd) for (c, p), kbd in zip(chains, kbds)]
    bcs = [bcol_ref[crow[c], :] for c in range(nch)]
    bes = [betac_ref[crow[c], :] for c in range(nch)]
    brs = [brow_ref[c] for c in range(nch)]
    side()
    a_s, tinvs, pws = [], [], []
    for n_, (c, p) in enumerate(chains):
        h1, h2 = 2 * p, 2 * p + 1
        pick = lambda v, l0: jnp.where(lo, v[:, l0 + h1:l0 + h1 + 1], v[:, l0 + h2:l0 + h2 + 1])
        diff = pick(bcs[c], AIN_LANE) - jnp.where(lo[0:1], brs[c][h1:h1 + 1, :], brs[c][h2:h2 + 1, :])
        dm = jnp.where(incl, jnp.exp(jnp.where(incl, diff, 0.0)), 0.0)
        qkm_ref[c, p] = (qks[n_] * dm).astype(BF16)
        a = jnp.where(strict, pick(bes[c], BETA_LANE) * kks[n_] * dm, 0.0)
        dblk = jnp.where(blk[INV_BASE], a, 0.0)
        a_s.append(a)
        tinvs.append(eye - dblk)
        pws.append(dblk.astype(BF16))
    side()
    n = 2
    while n < INV_BASE:
        pws = [_dot(p_, bdiag_packed(p_)).astype(BF16) for p_ in pws]
        tinvs = [t_ + _dot(t_.astype(BF16), bdiag_packed(p_)) for t_, p_ in zip(tinvs, pws)]
        side()
        n *= 2
    while n < CHUNK:
        es = [jnp.where(blk[2 * n] & ~blk[n], a, 0.0).astype(BF16) for a in a_s]
        tbs = [t_.astype(BF16) for t_ in tinvs]
        tbds = [bdiag_packed(tb) for tb in tbs]
        tes = [_dot(tb, bdiag_packed(e)).astype(BF16) for tb, e in zip(tbs, es)]
        side()
        tinvs = [t_ - _dot(te, tbd) for t_, te, tbd in zip(tinvs, tes, tbds)]
        side()
        n *= 2
    for t_, (c, p) in zip(tinvs, chains):
        tb = t_.astype(BF16)
        uv_ref[crow[c], pcols(p)] = _dot(tb, bdiag_wide(bv_ref[crow[c], pcols(p)]))
        wk_ref[crow[c], pcols(p)] = _dot(tb, bdiag_wide(bek_ref[crow[c], pcols(p)])).astype(BF16)
    side(len(side_tasks))

    def chunk_body(c, carry):
        rows = chunk_rows(c)
        ha, hb_ = range(GLA_HEADS), range(GDN_HEADS)
        lk = lambda h: slice(h * dk_a, (h + 1) * dk_a)
        lv = lambda h: slice(h * dv_a, (h + 1) * dv_a)
        hc = lambda h: slice(h * GDN_DK, (h + 1) * GDN_DK)
        sa = [sgla_ref[0, h] for h in ha]
        sb = [sgdn_ref[0, h] for h in hb_]
        v_a = [va_ref[rows, lv(h)] for h in ha]
        dec_a = bdec_ref[c]
        dv_all = dvec_ref[c]
        sab = [s_.astype(BF16) for s_ in sa]
        sbb = [s_.astype(BF16) for s_ in sb]
        wq = [_dot(jnp.concatenate([wk_ref[rows, hc(h)], qdec_ref[rows, hc(h)]], axis=0), sbb[h]) for h in hb_]
        ws = [w_[0:CHUNK] for w_ in wq]
        qsb = [w_[CHUNK:2 * CHUNK] for w_ in wq]
        att = [jnp.where(incl1, _dot_nt(qe_ref[rows, lk(h)], ke_ref[rows, lk(h)]), 0.0).astype(BF16) for h in ha]
        oi = [_dot(qd_ref[rows, lk(h)], sab[h]) for h in ha]
        kv = [_dot_tn(kd_ref[rows, lk(h)], v_a[h]) for h in ha]
        u = [(uv_ref[rows, hc(h)] - ws[h]).astype(BF16) for h in hb_]
        qku = [_dot(qkm_ref[c, p], bdiag_wide(jnp.concatenate([u[2 * p], u[2 * p + 1]], axis=1)))
               for p in range(npair)]
        o_b = [qsb[h] + qku[h // 2][:, (h % 2) * GDN_DV:(h % 2 + 1) * GDN_DV] for h in hb_]
        sb_new = [dv_all[h:h + 1, :] * sb[h] + _dot_tn(kdec_ref[rows, hc(h)], u[h]) for h in hb_]
        o_a = [_dot(att[h], v_a[h]) + oi[h] for h in ha]
        sa_new = []
        for h in ha:
            dcol = jnp.broadcast_to(dec_a[:, lk(h)], (dk_a, dk_a)).T
            sa_new.append(jnp.concatenate([dcol] * (dv_a // dk_a), axis=1) * sa[h] + kv[h])
        for h in ha:
            o = o_a[h]
            o = o * lax.rsqrt(jnp.mean(o * o, axis=-1, keepdims=True) + EPS) * gna
            oa_ref[rows, lv(h)] = (o * sza_ref[rows, lv(h)]).astype(BF16)
            sgla_ref[0, h] = sa_new[h]
        for h in hb_:
            o = o_b[h]
            o = o * lax.rsqrt(jnp.mean(o * o, axis=-1, keepdims=True) + EPS) * gnb
            ob_ref[rows, hc(h)] = (o * szb_ref[rows, hc(h)]).astype(BF16)
            sgdn_ref[0, h] = sb_new[h]
        return carry

    lax.fori_loop(0, nch, chunk_body, 0, unroll=True)

    merged = (jax.nn.sigmoid(proj(o_ga, d)) * _dot(oa_ref[...], wpa_ref[...])
              + jax.nn.sigmoid(proj(o_gb, d)) * _dot(ob_ref[...], wpb_ref[...]))
    out = _dot(merged.astype(BF16), wout_ref[...])
    xn = x_t[...] + mod_ref[0][:, 2 * d:3 * d] * out
    if final_norm:
        xn = xn * lax.rsqrt(jnp.mean(xn * xn, axis=-1, keepdims=True) + EPS) * gfin_ref[...]
    y_t[...] = xn


def _resident(shape):
    zeros = (0,) * len(shape)
    return pl.BlockSpec(shape, lambda b, t: zeros, pipeline_mode=pl.Buffered(1))


def _layer(x, mod, consts, states, *, tc, sub, n_valid, final_norm):
    bsz, t_len, d = x.shape
    nch = tc // CHUNK
    qk_a = d // 2
    has_state = states is not None
    const_specs = [_resident(c.shape) for c in consts]
    in_specs = [pl.BlockSpec((1, sub * tc, d), lambda b, t: (b, t, 0)),
                pl.BlockSpec((1, 1, 3 * d), lambda b, t: (b, 0, 0))] + const_specs
    args = [x, mod.reshape(bsz, 1, 3 * d)] + list(consts)
    state_shapes = [(bsz, GLA_HEADS, qk_a // GLA_HEADS, d // GLA_HEADS),
                    (bsz, GDN_HEADS, GDN_DK, GDN_DV),
                    (bsz, CONV_W - 1, 3 * d)]
    state_specs = [pl.BlockSpec((1,) + s[1:], lambda b, t, n=len(s): (b,) + (0,) * (n - 1)) for s in state_shapes]
    if has_state:
        in_specs += state_specs
        args += list(states)
    scratch = [
        pltpu.VMEM((tc, d), BF16),
        pltpu.VMEM((tc, qk_a), BF16), pltpu.VMEM((tc, qk_a), BF16),
        pltpu.VMEM((tc, qk_a), BF16), pltpu.VMEM((tc, qk_a), BF16),
        pltpu.VMEM((tc, d), BF16),
        pltpu.VMEM((tc, d), F32),
        pltpu.VMEM((nch, 1, qk_a), F32),
        pltpu.VMEM((HIST + tc, 3 * d), F32),
        pltpu.VMEM((tc, d), BF16), pltpu.VMEM((tc, d), BF16),
        pltpu.VMEM((tc, d), BF16), pltpu.VMEM((tc, d), BF16),
        pltpu.VMEM((tc, d), BF16), pltpu.VMEM((tc, d), BF16),
        pltpu.VMEM((tc, d), F32),
        pltpu.VMEM((tc, LANES), F32), pltpu.VMEM((tc, LANES), F32),
        pltpu.VMEM((nch, GDN_HEADS, 2 * CHUNK), F32),
        pltpu.VMEM((nch, GDN_HEADS, LANES), F32),
        pltpu.VMEM((tc, d), F32), pltpu.VMEM((tc, d), BF16),
        pltpu.VMEM((nch, GDN_HEADS // 2, CHUNK, 2 * CHUNK), BF16),
        pltpu.VMEM((tc, qk_a), F32),
        pltpu.VMEM((tc, d), BF16), pltpu.VMEM((tc, d), BF16),
    ]
    kern = functools.partial(_layer_kernel, sub=sub, tc=tc, n_valid=n_valid, has_state=has_state,
                             final_norm=final_norm, d_model=d)
    return pl.pallas_call(
        kern,
        grid=(bsz, t_len // (sub * tc)),
        in_specs=in_specs,
        out_specs=[pl.BlockSpec((1, sub * tc, d), lambda b, t: (b, t, 0))] + state_specs,
        out_shape=[jax.ShapeDtypeStruct(x.shape, F32)] + [jax.ShapeDtypeStruct(s, F32) for s in state_shapes],
        scratch_shapes=scratch,
        compiler_params=pltpu.CompilerParams(dimension_semantics=("arbitrary", "arbitrary"),
                                             vmem_limit_bytes=VMEM_LIMIT_BYTES),
        name="gla_gdn_layer",
    )(*args)


def _permute_cast_kernel(w_ref, o_ref, *, segments, pad):
    w = w_ref[...]
    parts = [w[:, a:b] for a, b in segments] + [jnp.zeros((w.shape[0], pad), w.dtype)]
    o_ref[...] = jnp.concatenate(parts, axis=1).astype(o_ref.dtype)


def _permute_cast(w, segments, pad):
    rows, cols = w.shape
    out_cols = sum(b - a for a, b in segments) + pad
    bm = PREP_ROWS
    return pl.pallas_call(
        functools.partial(_permute_cast_kernel, segments=segments, pad=pad),
        grid=(rows // bm,),
        in_specs=[pl.BlockSpec((bm, cols), lambda i: (i, 0))],
        out_specs=pl.BlockSpec((bm, out_cols), lambda i: (i, 0)),
        out_shape=jax.ShapeDtypeStruct((rows, out_cols), BF16),
        name="permute_cast",
    )(w)


def _layer_consts(g_norm1, w_in, w_gk2, b_gk, w_conv, a_log, dt_bias, g_norm_a, g_norm_b,
                  w_pa, w_pb, w_out, g_final):
    d = w_in.shape[0]
    qk_a = d // 2
    o_gk = 2 * qk_a + 2 * d
    o_qkv = o_gk + GLA_RANK
    o_zb = o_qkv + 3 * d
    o_beta = o_zb + d
    o_a = o_beta + GDN_HEADS
    o_ga = o_a + GDN_HEADS
    pad = LANES - (GLA_RANK + 2 * GDN_HEADS)
    w_perm = _permute_cast(w_in, ((0, o_gk), (o_qkv, o_beta), (o_ga, o_ga + 2 * d), (o_gk, o_qkv), (o_beta, o_ga)), pad)
    w_at = w_in[:, o_a:o_ga].T.astype(BF16)
    w_gk = jnp.zeros((LANES, qk_a), F32).at[0:GLA_RANK].set(w_gk2).astype(BF16)
    lane_vec = lambda v: jnp.zeros((1, LANES), F32).at[0, AIN_LANE:AIN_LANE + GDN_HEADS].set(v)
    col_vec = lambda v: jnp.broadcast_to(v.reshape(GDN_HEADS, 1), (GDN_HEADS, CHUNK)).astype(F32)
    tu = jnp.arange(CHUNK)
    ut = jnp.tile(tu[:, None] <= tu[None, :], (1, 2)).astype(BF16)
    return [g_norm1.reshape(1, d), w_perm, w_at, w_gk, b_gk.reshape(1, qk_a), w_conv,
            lane_vec(a_log), lane_vec(dt_bias), col_vec(a_log), col_vec(dt_bias),
            g_norm_a.reshape(1, -1), g_norm_b.reshape(1, -1),
            w_pa.astype(BF16), w_pb.astype(BF16), w_out.astype(BF16), g_final.reshape(1, d), ut]


def _tile_consts(consts, tc):
    tt = jnp.arange(tc)
    bd = ((tt[:, None] // CHUNK == tt[None, :] // CHUNK) & (tt[None, :] <= tt[:, None])).astype(BF16)
    return consts[:-1] + [bd, consts[-1]]


PROMPT_TILE = 256
PROMPT_SUB = 2


def kernel(x_prompt, x_sample, c_prompt, c_sample, state_gla, state_gdn, cache_conv_gdn, w_ada, b_ada, g_norm1, w_in, w_gk2, b_gk, w_conv, a_log, dt_bias, g_norm_a, g_norm_b, w_pa, w_pb, w_out, g_final):
    depth = w_in.shape[0]
    bp, tp, _ = x_prompt.shape
    bs, ts, _ = x_sample.shape
    assert tp % (PROMPT_SUB * PROMPT_TILE) == 0 and CONV_W - 1 <= ts <= CHUNK
    hp = x_prompt
    hs = jnp.pad(x_sample, ((0, 0), (0, CHUNK - ts), (0, 0)))
    outs_p, outs_s = [], []
    for layer in range(depth):
        last = layer == depth - 1
        mod = _adaln_mod(jnp.concatenate([c_prompt, c_sample], axis=0), w_ada[layer], b_ada[layer])
        lw = (g_norm1[layer], w_in[layer], w_gk2[layer], b_gk[layer], w_conv[layer], a_log[layer],
              dt_bias[layer], g_norm_a[layer], g_norm_b[layer], w_pa[layer], w_pb[layer], w_out[layer], g_final)
        consts = _layer_consts(*lw)
        hp, *st_p = _layer(hp, mod[:bp], _tile_consts(consts, PROMPT_TILE), None,
                           tc=PROMPT_TILE, sub=PROMPT_SUB, n_valid=PROMPT_TILE, final_norm=last)
        st_in = (state_gla[layer], state_gdn[layer], cache_conv_gdn[layer])
        hs, *st_s = _layer(hs, mod[bp:], _tile_consts(consts, CHUNK), st_in,
                           tc=CHUNK, sub=1, n_valid=ts, final_norm=last)
        outs_p.append(st_p)
        outs_s.append(st_s)

    stack = lambda outs, i: jnp.stack([o[i] for o in outs])
    return (hp, hs[:, :ts], stack(outs_p, 0), stack(outs_p, 1), stack(outs_p, 2),
            stack(outs_s, 0), stack(outs_s, 1), stack(outs_s, 2))
```

```python
import functools

import jax
import jax.numpy as jnp
from jax import lax
from jax.experimental import pallas as pl
from jax.experimental.pallas import tpu as pltpu

F32 = jnp.float32
BF16 = jnp.bfloat16

CHUNK = 64
EPS = 1e-6
GLA_HEADS = 4
GLA_RANK = 16
GLA_GATE_NORM = 16.0
GDN_HEADS = 8
GDN_DK = 128
GDN_DV = 128
CONV_W = 4
LANES = 128
HIST = 8
INV_BASE = 8
PREP_ROWS = 128
BETA_LANE = GLA_RANK
AIN_LANE = GLA_RANK + GDN_HEADS
VMEM_LIMIT_BYTES = 60 * 1024 * 1024


def _dot(a, b):
    return jnp.dot(a, b, preferred_element_type=F32)


def _dot_nt(a, b):
    return lax.dot_general(a, b, (((1,), (1,)), ((), ())), preferred_element_type=F32)


def _dot_tn(a, b):
    return lax.dot_general(a, b, (((0,), (0,)), ((), ())), preferred_element_type=F32)


def _split2(x):
    hi = x.astype(BF16)
    return hi, (x - hi.astype(F32)).astype(BF16)


def _split3(x):
    hi = x.astype(BF16)
    r = x - hi.astype(F32)
    mid = r.astype(BF16)
    lo = (r - mid.astype(F32)).astype(BF16)
    return hi, mid, lo


def _softplus(x):
    return jnp.maximum(x, 0.0) + jnp.log1p(jnp.exp(-jnp.abs(x)))


def _log_sigmoid(x):
    return jnp.minimum(x, 0.0) - jnp.log1p(jnp.exp(-jnp.abs(x)))


def _silu(x):
    return x * jax.nn.sigmoid(x)


def _mod_kernel(c_ref, w_ref, b_ref, o_ref):
    s = _silu(c_ref[...]).astype(BF16)
    o_ref[...] = _dot(s, w_ref[...].astype(BF16)) + b_ref[...]


def _adaln_mod(c, w_ada, b_ada):
    n, d = c.shape
    d3 = w_ada.shape[1]
    bn = 512
    return pl.pallas_call(
        _mod_kernel,
        grid=(d3 // bn,),
        in_specs=[pl.BlockSpec((n, d), lambda j: (0, 0)),
                  pl.BlockSpec((d, bn), lambda j: (0, j)),
                  pl.BlockSpec((1, bn), lambda j: (0, j))],
        out_specs=pl.BlockSpec((n, bn), lambda j: (0, j)),
        out_shape=jax.ShapeDtypeStruct((n, d3), F32),
        name="adaln_mod",
    )(c, w_ada, b_ada.reshape(1, d3))


def _layer_kernel(*refs, sub, **static):
    for s in range(sub):
        _tile_body(refs, s, **static)


def _tile_body(refs, s, *, tc, n_valid, has_state, final_norm, d_model):
    d = d_model
    qk_a = d // 2
    dk_a = qk_a // GLA_HEADS
    dv_a = d // GLA_HEADS
    nch = tc // CHUNK
    o_qa, o_ka, o_va, o_za = 0, qk_a, 2 * qk_a, 2 * qk_a + d
    o_qb = o_za + d
    o_kb, o_vb = o_qb + d, o_qb + 2 * d
    o_zb = o_qb + 3 * d
    o_ga, o_gb = o_zb + d, o_zb + 2 * d
    o_sm = o_gb + d

    it = iter(refs)
    x_ref, mod_ref, g1_ref, win_ref, wat_ref, wgk_ref, bgk_ref, wconv_ref = (next(it) for _ in range(8))
    alane_ref, dlane_ref, acol_ref, dcol_ref = (next(it) for _ in range(4))
    gna_ref, gnb_ref, wpa_ref, wpb_ref, wout_ref, gfin_ref, bd_ref, ut_ref = (next(it) for _ in range(8))
    if has_state:
        sgla_in, sgdn_in, conv_in = (next(it) for _ in range(3))
    y_ref, sgla_ref, sgdn_ref, conv_ref = (next(it) for _ in range(4))
    (hb_ref, qe_ref, ke_ref, qd_ref, kd_ref, va_ref, sza_ref, bdec_ref, ubuf_ref,
     qn_ref, kn_ref, qdec_ref, kdec_ref, bv_ref, bek_ref, szb_ref,
     bcol_ref, betac_ref, brow_ref, dvec_ref, uv_ref, wk_ref, qkm_ref, ba_ref, oa_ref, ob_ref) = (next(it) for _ in range(26))

    x_t = x_ref.at[0, pl.ds(s * tc, tc)]
    y_t = y_ref.at[0, pl.ds(s * tc, tc)]

    def _init():
        if has_state:
            sgla_ref[...] = sgla_in[...]
            sgdn_ref[...] = sgdn_in[...]
            ubuf_ref[HIST - (CONV_W - 1):HIST, :] = conv_in[0]
        else:
            sgla_ref[...] = jnp.zeros_like(sgla_ref)
            sgdn_ref[...] = jnp.zeros_like(sgdn_ref)
            ubuf_ref[0:HIST, :] = jnp.zeros((HIST, ubuf_ref.shape[1]), F32)

    if s == 0:
        pl.when(pl.program_id(1) == 0)(_init)

    masked = n_valid < tc
    if masked:
        rowmask = lax.broadcasted_iota(jnp.int32, (tc, 1), 0) < n_valid

    def mrow(v):
        return jnp.where(rowmask, v, 0.0) if masked else v

    x = x_t[...]
    mod = mod_ref[0]
    shift, scale = mod[:, 0:d], mod[:, d:2 * d]
    hn = x * lax.rsqrt(jnp.mean(x * x, axis=-1, keepdims=True) + EPS) * g1_ref[...]
    hb_ref[...] = (hn * (1.0 + scale) + shift).astype(BF16)

    def proj(c0, width):
        return _dot(hb_ref[...], win_ref[:, c0:c0 + width])

    pair_w = 2 * GDN_DK
    half = d // 2

    def gdn_proj(jp, parts=(0, 1, 2)):
        for j in parts:
            c0 = j * d + jp * pair_w
            ubuf_ref[HIST:HIST + tc, c0:c0 + pair_w] = proj(o_qb + c0, pair_w)

    def va_task(i):
        va_ref[:, i * half:(i + 1) * half] = mrow(proj(o_va + i * half, half)).astype(BF16)

    def silu_task(dst_ref, c0, i):
        dst_ref[:, i * half:(i + 1) * half] = _silu(proj(c0 + i * half, half))

    ps = proj(o_sm, LANES)
    arows = [_dot_nt(wat_ref[...], hb_ref[c * CHUNK:(c + 1) * CHUNK, :]) for c in range(nch)]
    gdn_proj(0, (0,))
    lane = lax.broadcasted_iota(jnp.int32, (1, LANES), 1)
    betac_ref[...] = mrow(jax.nn.sigmoid(ps))
    g_col = -jnp.exp(alane_ref[...]) * _softplus(ps + dlane_ref[...])
    g_col = mrow(jnp.where((lane >= AIN_LANE) & (lane < AIN_LANE + GDN_HEADS), g_col, 0.0))
    g_rows = []
    for c in range(nch):
        g_row = -jnp.exp(acol_ref[...]) * _softplus(arows[c] + dcol_ref[...])
        if masked:
            colmask = (lax.broadcasted_iota(jnp.int32, (1, CHUNK), 1) + c * CHUNK) < n_valid
            g_row = jnp.where(colmask, g_row, 0.0)
        g_rows.append(g_row)
    gdn_proj(0, (1,))
    bd = bd_ref[...]
    ut = ut_ref[...]
    gh, gm, gl = _split3(g_col)
    b_col = _dot(bd, gh) + _dot(bd, gm) + _dot(bd, gl)
    b_rows = []
    for c in range(nch):
        rh, rm, rl = _split3(g_rows[c])
        b_rows.append(_dot(rh, ut) + _dot(rm, ut) + _dot(rl, ut))
    gdn_proj(0, (2,))
    va_task(0)
    bcol_ref[...] = b_col
    for c in range(nch):
        brow_ref[c] = b_rows[c]
        dvec_ref[c] = jnp.broadcast_to(jnp.exp(b_rows[c][:, CHUNK - 1:CHUNK]), (GDN_HEADS, LANES))
    e_b = jnp.exp(b_col)
    b_col3 = b_col.reshape(nch, CHUNK, LANES)
    e_lb = jnp.exp(b_col3[:, CHUNK - 1:CHUNK, :] - b_col3).reshape(tc, LANES)
    beta = betac_ref[...]
    va_task(1)

    def conv(c0, width):
        cols = slice(c0, c0 + width)
        full = ubuf_ref[:, cols]
        acc = full[HIST:] * wconv_ref[CONV_W - 1:CONV_W, cols]
        for i in range(CONV_W - 1):
            back = CONV_W - 1 - i
            acc = acc + pltpu.roll(full, back, 0)[HIST:] * wconv_ref[i:i + 1, cols]
        return _silu(acc)

    def gdn_prep(h):
        cols = slice(h * GDN_DK, (h + 1) * GDN_DK)
        qh, kh_, vh = (conv(j * d + h * GDN_DK, GDN_DK) for j in range(3))
        be_h = beta[:, BETA_LANE + h:BETA_LANE + h + 1]
        eb_h = e_b[:, AIN_LANE + h:AIN_LANE + h + 1]
        elb_h = e_lb[:, AIN_LANE + h:AIN_LANE + h + 1]
        qh = mrow(qh * lax.rsqrt(jnp.sum(qh * qh, axis=-1, keepdims=True) + EPS) * (GDN_DK ** -0.5))
        kh_ = mrow(kh_ * lax.rsqrt(jnp.sum(kh_ * kh_, axis=-1, keepdims=True) + EPS))
        vh = mrow(vh)
        qn_ref[:, cols] = qh.astype(BF16)
        kn_ref[:, cols] = kh_.astype(BF16)
        qdec_ref[:, cols] = (qh * eb_h).astype(BF16)
        kdec_ref[:, cols] = (kh_ * elb_h).astype(BF16)
        bv_ref[:, cols] = (be_h * vh).astype(BF16)
        bek_ref[:, cols] = ((be_h * eb_h) * kh_).astype(BF16)

    def gla_gate_task():
        gk = _log_sigmoid(_dot(ps.astype(BF16), wgk_ref[...]) + bgk_ref[...]) * (1.0 / GLA_GATE_NORM)
        gk = mrow(gk)
        kh, kl = _split2(gk)
        ba_ref[...] = _dot(bd, kh) + _dot(bd, kl)

    def gla_qk_task(which):
        b_a = ba_ref[...].reshape(nch, CHUNK, qk_a)
        b_mid = b_a[:, CHUNK // 2 - 1:CHUNK // 2, :]
        b_last = b_a[:, CHUNK - 1:CHUNK, :]
        if which == 0:
            bdec_ref[...] = jnp.exp(b_last)
            qa = mrow(proj(o_qa, qk_a) * (dk_a ** -0.5)).reshape(nch, CHUNK, qk_a)
            qe_ref[...] = (qa * jnp.exp(b_a - b_mid)).reshape(tc, qk_a).astype(BF16)
            qd_ref[...] = (qa * jnp.exp(b_a)).reshape(tc, qk_a).astype(BF16)
        else:
            ka = mrow(proj(o_ka, qk_a)).reshape(nch, CHUNK, qk_a)
            ke_ref[...] = (ka * jnp.exp(b_mid - b_a)).reshape(tc, qk_a).astype(BF16)
            kd_ref[...] = (ka * jnp.exp(b_last - b_a)).reshape(tc, qk_a).astype(BF16)

    side_tasks = [gla_gate_task,
                  functools.partial(gla_qk_task, 0), functools.partial(gla_qk_task, 1),
                  functools.partial(silu_task, sza_ref, o_za, 0), functools.partial(silu_task, sza_ref, o_za, 1),
                  functools.partial(silu_task, szb_ref, o_zb, 0), functools.partial(silu_task, szb_ref, o_zb, 1)]

    def side(n=1):
        for _ in range(n):
            if side_tasks:
                side_tasks.pop(0)()

    n_pairs = GDN_HEADS // 2
    for jp in range(n_pairs):
        more = jp + 1 < n_pairs
        if more:
            gdn_proj(jp + 1, (0,))
        gdn_prep(2 * jp)
        if more:
            gdn_proj(jp + 1, (1,))
        side()
        gdn_prep(2 * jp + 1)
        if more:
            gdn_proj(jp + 1, (2,))
        else:
            side()

    tail = ubuf_ref[HIST + n_valid - (CONV_W - 1):HIST + n_valid, :]
    ubuf_ref[HIST - (CONV_W - 1):HIST, :] = tail
    conv_ref[0] = tail

    npair = GDN_HEADS // 2
    ri = lax.broadcasted_iota(jnp.int32, (CHUNK, 2 * CHUNK), 0)
    li = lax.broadcasted_iota(jnp.int32, (CHUNK, 2 * CHUNK), 1)
    ci = li & (CHUNK - 1)
    lo = li < CHUNK
    incl = ri >= ci
    strict = ri > ci
    eye = jnp.where(ri == ci, 1.0, 0.0).astype(F32)
    blk = {}
    s_ = INV_BASE
    while s_ <= CHUNK:
        sh = s_.bit_length() - 1
        blk[s_] = (ri >> sh) == (ci >> sh)
        s_ *= 2
    incl1 = (lax.broadcasted_iota(jnp.int32, (CHUNK, CHUNK), 0)
             >= lax.broadcasted_iota(jnp.int32, (CHUNK, CHUNK), 1))
    gna = gna_ref[...]
    gnb = gnb_ref[...]
    zblk = jnp.zeros((CHUNK, LANES), BF16)

    def bdiag_packed(y):
        return jnp.concatenate([jnp.where(lo, y, 0), jnp.where(lo, 0, y)], axis=0)

    def bdiag_wide(y):
        return jnp.concatenate([jnp.concatenate([y[:, 0:LANES], zblk], axis=1),
                                jnp.concatenate([zblk, y[:, LANES:2 * LANES]], axis=1)], axis=0)

    def chunk_rows(c):
        return pl.ds(pl.multiple_of(c * CHUNK, CHUNK), CHUNK)

    chains = [(c, p) for c in range(nch) for p in range(npair)]
    crow = [slice(c * CHUNK, (c + 1) * CHUNK) for c in range(nch)]
    pcols = lambda p: slice(p * pair_w, (p + 1) * pair_w)
    kbds = [bdiag_wide(kn_ref[crow[c], pcols(p)]) for c, p in chains]
    kks = [_dot_nt(kn_ref[crow[c], pcols(p)], kbd) for (c, p), kbd in zip(chains, kbds)]
    qks = [_dot_nt(qn_ref[crow[c], pcols(p)], kbd) for (c, p), kbd in zip(chains, kbds)]
    bcs = [bcol_ref[crow[c], :] for c in range(nch)]
    bes = [betac_ref[crow[c], :] for c in range(nch)]
    brs = [brow_ref[c] for c in range(nch)]
    side()
    a_s, tinvs, pws = [], [], []
    for n_, (c, p) in enumerate(chains):
        h1, h2 = 2 * p, 2 * p + 1
        pick = lambda v, l0: jnp.where(lo, v[:, l0 + h1:l0 + h1 + 1], v[:, l0 + h2:l0 + h2 + 1])
        diff = pick(bcs[c], AIN_LANE) - jnp.where(lo[0:1], brs[c][h1:h1 + 1, :], brs[c][h2:h2 + 1, :])
        dm = jnp.where(incl, jnp.exp(jnp.where(incl, diff, 0.0)), 0.0)
        qkm_ref[c, p] = (qks[n_] * dm).astype(BF16)
        a = jnp.where(strict, pick(bes[c], BETA_LANE) * kks[n_] * dm, 0.0)
        dblk = jnp.where(blk[INV_BASE], a, 0.0)
        a_s.append(a)
        tinvs.append(eye - dblk)
        pws.append(dblk.astype(BF16))
    side()
    n = 2
    while n < INV_BASE:
        pws = [_dot(p_, bdiag_packed(p_)).astype(BF16) for p_ in pws]
        tinvs = [t_ + _dot(t_.astype(BF16), bdiag_packed(p_)) for t_, p_ in zip(tinvs, pws)]
        side()
        n *= 2
    while n < CHUNK:
        es = [jnp.where(blk[2 * n] & ~blk[n], a, 0.0).astype(BF16) for a in a_s]
        tbs = [t_.astype(BF16) for t_ in tinvs]
        tbds = [bdiag_packed(tb) for tb in tbs]
        tes = [_dot(tb, bdiag_packed(e)).astype(BF16) for tb, e in zip(tbs, es)]
        side()
        tinvs = [t_ - _dot(te, tbd) for t_, te, tbd in zip(tinvs, tes, tbds)]
        side()
        n *= 2
    for t_, (c, p) in zip(tinvs, chains):
        tb = t_.astype(BF16)
        uv_ref[crow[c], pcols(p)] = _dot(tb, bdiag_wide(bv_ref[crow[c], pcols(p)]))
        wk_ref[crow[c], pcols(p)] = _dot(tb, bdiag_wide(bek_ref[crow[c], pcols(p)])).astype(BF16)
    side(len(side_tasks))

    def chunk_body(c, carry):
        rows = chunk_rows(c)
        ha, hb_ = range(GLA_HEADS), range(GDN_HEADS)
        lk = lambda h: slice(h * dk_a, (h + 1) * dk_a)
        lv = lambda h: slice(h * dv_a, (h + 1) * dv_a)
        hc = lambda h: slice(h * GDN_DK, (h + 1) * GDN_DK)
        sa = [sgla_ref[0, h] for h in ha]
        sb = [sgdn_ref[0, h] for h in hb_]
        v_a = [va_ref[rows, lv(h)] for h in ha]
        dec_a = bdec_ref[c]
        dv_all = dvec_ref[c]
        sab = [s_.astype(BF16) for s_ in sa]
        sbb = [s_.astype(BF16) for s_ in sb]
        wq = [_dot(jnp.concatenate([wk_ref[rows, hc(h)], qdec_ref[rows, hc(h)]], axis=0), sbb[h]) for h in hb_]
        ws = [w_[0:CHUNK] for w_ in wq]
        qsb = [w_[CHUNK:2 * CHUNK] for w_ in wq]
        att = [jnp.where(incl1, _dot_nt(qe_ref[rows, lk(h)], ke_ref[rows, lk(h)]), 0.0).astype(BF16) for h in ha]
        oi = [_dot(qd_ref[rows, lk(h)], sab[h]) for h in ha]
        kv = [_dot_tn(kd_ref[rows, lk(h)], v_a[h]) for h in ha]
        u = [(uv_ref[rows, hc(h)] - ws[h]).astype(BF16) for h in hb_]
        qku = [_dot(qkm_ref[c, p], bdiag_wide(jnp.concatenate([u[2 * p], u[2 * p + 1]], axis=1)))
               for p in range(npair)]
        o_b = [qsb[h] + qku[h // 2][:, (h % 2) * GDN_DV:(h % 2 + 1) * GDN_DV] for h in hb_]
        sb_new = [dv_all[h:h + 1, :] * sb[h] + _dot_tn(kdec_ref[rows, hc(h)], u[h]) for h in hb_]
        o_a = [_dot(att[h], v_a[h]) + oi[h] for h in ha]
        sa_new = []
        for h in ha:
            dcol = jnp.broadcast_to(dec_a[:, lk(h)], (dk_a, dk_a)).T
            sa_new.append(jnp.concatenate([dcol] * (dv_a // dk_a), axis=1) * sa[h] + kv[h])
        for h in ha:
            o = o_a[h]
            o = o * lax.rsqrt(jnp.mean(o * o, axis=-1, keepdims=True) + EPS) * gna
            oa_ref[rows, lv(h)] = (o * sza_ref[rows, lv(h)]).astype(BF16)
            sgla_ref[0, h] = sa_new[h]
        for h in hb_:
            o = o_b[h]
            o = o * lax.rsqrt(jnp.mean(o * o, axis=-1, keepdims=True) + EPS) * gnb
            ob_ref[rows, hc(h)] = (o * szb_ref[rows, hc(h)]).astype(BF16)
            sgdn_ref[0, h] = sb_new[h]
        return carry

    lax.fori_loop(0, nch, chunk_body, 0, unroll=True)

    merged = (jax.nn.sigmoid(proj(o_ga, d)) * _dot(oa_ref[...], wpa_ref[...])
              + jax.nn.sigmoid(proj(o_gb, d)) * _dot(ob_ref[...], wpb_ref[...]))
    out = _dot(merged.astype(BF16), wout_ref[...])
    xn = x_t[...] + mod_ref[0][:, 2 * d:3 * d] * out
    if final_norm:
        xn = xn * lax.rsqrt(jnp.mean(xn * xn, axis=-1, keepdims=True) + EPS) * gfin_ref[...]
    y_t[...] = xn


def _resident(shape):
    zeros = (0,) * len(shape)
    return pl.BlockSpec(shape, lambda b, t: zeros, pipeline_mode=pl.Buffered(1))


def _layer(x, mod, consts, states, *, tc, sub, n_valid, final_norm):
    bsz, t_len, d = x.shape
    nch = tc // CHUNK
    qk_a = d // 2
    has_state = states is not None
    const_specs = [_resident(c.shape) for c in consts]
    in_specs = [pl.BlockSpec((1, sub * tc, d), lambda b, t: (b, t, 0)),
                pl.BlockSpec((1, 1, 3 * d), lambda b, t: (b, 0, 0))] + const_specs
    args = [x, mod.reshape(bsz, 1, 3 * d)] + list(consts)
    state_shapes = [(bsz, GLA_HEADS, qk_a // GLA_HEADS, d // GLA_HEADS),
                    (bsz, GDN_HEADS, GDN_DK, GDN_DV),
                    (bsz, CONV_W - 1, 3 * d)]
    state_specs = [pl.BlockSpec((1,) + s[1:], lambda b, t, n=len(s): (b,) + (0,) * (n - 1)) for s in state_shapes]
    if has_state:
        in_specs += state_specs
        args += list(states)
    scratch = [
        pltpu.VMEM((tc, d), BF16),
        pltpu.VMEM((tc, qk_a), BF16), pltpu.VMEM((tc, qk_a), BF16),
        pltpu.VMEM((tc, qk_a), BF16), pltpu.VMEM((tc, qk_a), BF16),
        pltpu.VMEM((tc, d), BF16),
        pltpu.VMEM((tc, d), F32),
        pltpu.VMEM((nch, 1, qk_a), F32),
        pltpu.VMEM((HIST + tc, 3 * d), F32),
        pltpu.VMEM((tc, d), BF16), pltpu.VMEM((tc, d), BF16),
        pltpu.VMEM((tc, d), BF16), pltpu.VMEM((tc, d), BF16),
        pltpu.VMEM((tc, d), BF16), pltpu.VMEM((tc, d), BF16),
        pltpu.VMEM((tc, d), F32),
        pltpu.VMEM((tc, LANES), F32), pltpu.VMEM((tc, LANES), F32),
        pltpu.VMEM((nch, GDN_HEADS, 2 * CHUNK), F32),
        pltpu.VMEM((nch, GDN_HEADS, LANES), F32),
        pltpu.VMEM((tc, d), F32), pltpu.VMEM((tc, d), BF16),
        pltpu.VMEM((nch, GDN_HEADS // 2, CHUNK, 2 * CHUNK), BF16),
        pltpu.VMEM((tc, qk_a), F32),
        pltpu.VMEM((tc, d), BF16), pltpu.VMEM((tc, d), BF16),
    ]
    kern = functools.partial(_layer_kernel, sub=sub, tc=tc, n_valid=n_valid, has_state=has_state,
                             final_norm=final_norm, d_model=d)
    return pl.pallas_call(
        kern,
        grid=(bsz, t_len // (sub * tc)),
        in_specs=in_specs,
        out_specs=[pl.BlockSpec((1, sub * tc, d), lambda b, t: (b, t, 0))] + state_specs,
        out_shape=[jax.ShapeDtypeStruct(x.shape, F32)] + [jax.ShapeDtypeStruct(s, F32) for s in state_shapes],
        scratch_shapes=scratch,
        compiler_params=pltpu.CompilerParams(dimension_semantics=("arbitrary", "arbitrary"),
                                             vmem_limit_bytes=VMEM_LIMIT_BYTES),
        name="gla_gdn_layer",
    )(*args)


def _permute_cast_kernel(w_ref, o_ref, *, segments, pad):
    w = w_ref[...]
    parts = [w[:, a:b] for a, b in segments] + [jnp.zeros((w.shape[0], pad), w.dtype)]
    o_ref[...] = jnp.concatenate(parts, axis=1).astype(o_ref.dtype)


def _permute_cast(w, segments, pad):
    rows, cols = w.shape
    out_cols = sum(b - a for a, b in segments) + pad
    bm = PREP_ROWS
    return pl.pallas_call(
        functools.partial(_permute_cast_kernel, segments=segments, pad=pad),
        grid=(rows // bm,),
        in_specs=[pl.BlockSpec((bm, cols), lambda i: (i, 0))],
        out_specs=pl.BlockSpec((bm, out_cols), lambda i: (i, 0)),
        out_shape=jax.ShapeDtypeStruct((rows, out_cols), BF16),
        name="permute_cast",
    )(w)


def _layer_consts(g_norm1, w_in, w_gk2, b_gk, w_conv, a_log, dt_bias, g_norm_a, g_norm_b,
                  w_pa, w_pb, w_out, g_final):
    d = w_in.shape[0]
    qk_a = d // 2
    o_gk = 2 * qk_a + 2 * d
    o_qkv = o_gk + GLA_RANK
    o_zb = o_qkv + 3 * d
    o_beta = o_zb + d
    o_a = o_beta + GDN_HEADS
    o_ga = o_a + GDN_HEADS
    pad = LANES - (GLA_RANK + 2 * GDN_HEADS)
    w_perm = _permute_cast(w_in, ((0, o_gk), (o_qkv, o_beta), (o_ga, o_ga + 2 * d), (o_gk, o_qkv), (o_beta, o_ga)), pad)
    w_at = w_in[:, o_a:o_ga].T.astype(BF16)
    w_gk = jnp.zeros((LANES, qk_a), F32).at[0:GLA_RANK].set(w_gk2).astype(BF16)
    lane_vec = lambda v: jnp.zeros((1, LANES), F32).at[0, AIN_LANE:AIN_LANE + GDN_HEADS].set(v)
    col_vec = lambda v: jnp.broadcast_to(v.reshape(GDN_HEADS, 1), (GDN_HEADS, CHUNK)).astype(F32)
    tu = jnp.arange(CHUNK)
    ut = jnp.tile(tu[:, None] <= tu[None, :], (1, 2)).astype(BF16)
    return [g_norm1.reshape(1, d), w_perm, w_at, w_gk, b_gk.reshape(1, qk_a), w_conv,
            lane_vec(a_log), lane_vec(dt_bias), col_vec(a_log), col_vec(dt_bias),
            g_norm_a.reshape(1, -1), g_norm_b.reshape(1, -1),
            w_pa.astype(BF16), w_pb.astype(BF16), w_out.astype(BF16), g_final.reshape(1, d), ut]


def _tile_consts(consts, tc):
    tt = jnp.arange(tc)
    bd = ((tt[:, None] // CHUNK == tt[None, :] // CHUNK) & (tt[None, :] <= tt[:, None])).astype(BF16)
    return consts[:-1] + [bd, consts[-1]]


PROMPT_TILE = 256
PROMPT_SUB = 2


def kernel(x_prompt, x_sample, c_prompt, c_sample, state_gla, state_gdn, cache_conv_gdn, w_ada, b_ada, g_norm1, w_in, w_gk2, b_gk, w_conv, a_log, dt_bias, g_norm_a, g_norm_b, w_pa, w_pb, w_out, g_final):
    depth = w_in.shape[0]
    bp, tp, _ = x_prompt.shape
    bs, ts, _ = x_sample.shape
    assert tp % (PROMPT_SUB * PROMPT_TILE) == 0 and CONV_W - 1 <= ts <= CHUNK
    hp = x_prompt
    hs = jnp.pad(x_sample, ((0, 0), (0, CHUNK - ts), (0, 0)))
    outs_p, outs_s = [], []
    for layer in range(depth):
        last = layer == depth - 1
        mod = _adaln_mod(jnp.concatenate([c_prompt, c_sample], axis=0), w_ada[layer], b_ada[layer])
        lw = (g_norm1[layer], w_in[layer], w_gk2[layer], b_gk[layer], w_conv[layer], a_log[layer],
              dt_bias[layer], g_norm_a[layer], g_norm_b[layer], w_pa[layer], w_pb[layer], w_out[layer], g_final)
        consts = _layer_consts(*lw)
        hp, *st_p = _layer(hp, mod[:bp], _tile_consts(consts, PROMPT_TILE), None,
                           tc=PROMPT_TILE, sub=PROMPT_SUB, n_valid=PROMPT_TILE, final_norm=last)
        st_in = (state_gla[layer], state_gdn[layer], cache_conv_gdn[layer])
        hs, *st_s = _layer(hs, mod[bp:], _tile_consts(consts, CHUNK), st_in,
                           tc=CHUNK, sub=1, n_valid=ts, final_norm=last)
        outs_p.append(st_p)
        outs_s.append(st_s)

    stack = lambda outs, i: jnp.stack([o[i] for o in outs])
    return (hp, hs[:, :ts], stack(outs_p, 0), stack(outs_p, 1), stack(outs_p, 2),
            stack(outs_s, 0), stack(outs_s, 1), stack(outs_s, 2))
```

```python
import functools

import jax
import jax.numpy as jnp
from jax import lax
from jax.experimental import pallas as pl
from jax.experimental.pallas import tpu as pltpu

F32 = jnp.float32
BF16 = jnp.bfloat16

CHUNK = 64
EPS = 1e-6
GLA_HEADS = 4
GLA_RANK = 16
GLA_GATE_NORM = 16.0
GDN_HEADS = 8
GDN_DK = 128
GDN_DV = 128
CONV_W = 4
LANES = 128
HIST = 8
INV_BASE = 8
PREP_ROWS = 128
BETA_LANE = GLA_RANK
AIN_LANE = GLA_RANK + GDN_HEADS
VMEM_LIMIT_BYTES = 60 * 1024 * 1024


def _dot(a, b):
    return jnp.dot(a, b, preferred_element_type=F32)


def _dot_nt(a, b):
    return lax.dot_general(a, b, (((1,), (1,)), ((), ())), preferred_element_type=F32)


def _dot_tn(a, b):
    return lax.dot_general(a, b, (((0,), (0,)), ((), ())), preferred_element_type=F32)


def _split2(x):
    hi = x.astype(BF16)
    return hi, (x - hi.astype(F32)).astype(BF16)


def _split3(x):
    hi = x.astype(BF16)
    r = x - hi.astype(F32)
    mid = r.astype(BF16)
    lo = (r - mid.astype(F32)).astype(BF16)
    return hi, mid, lo


def _softplus(x):
    return jnp.maximum(x, 0.0) + jnp.log1p(jnp.exp(-jnp.abs(x)))


def _log_sigmoid(x):
    return jnp.minimum(x, 0.0) - jnp.log1p(jnp.exp(-jnp.abs(x)))


def _silu(x):
    return x * jax.nn.sigmoid(x)


def _mod_kernel(c_ref, w_ref, b_ref, o_ref):
    s = _silu(c_ref[...]).astype(BF16)
    o_ref[...] = _dot(s, w_ref[...].astype(BF16)) + b_ref[...]


def _adaln_mod(c, w_ada, b_ada):
    n, d = c.shape
    d3 = w_ada.shape[1]
    bn = 512
    return pl.pallas_call(
        _mod_kernel,
        grid=(d3 // bn,),
        in_specs=[pl.BlockSpec((n, d), lambda j: (0, 0)),
                  pl.BlockSpec((d, bn), lambda j: (0, j)),
                  pl.BlockSpec((1, bn), lambda j: (0, j))],
        out_specs=pl.BlockSpec((n, bn), lambda j: (0, j)),
        out_shape=jax.ShapeDtypeStruct((n, d3), F32),
        name="adaln_mod",
    )(c, w_ada, b_ada.reshape(1, d3))


def _layer_kernel(*refs, sub, **static):
    for s in range(sub):
        _tile_body(refs, s, **static)


def _tile_body(refs, s, *, tc, n_valid, has_state, final_norm, d_model):
    d = d_model
    qk_a = d // 2
    dk_a = qk_a // GLA_HEADS
    dv_a = d // GLA_HEADS
    nch = tc // CHUNK
    o_qa, o_ka, o_va, o_za = 0, qk_a, 2 * qk_a, 2 * qk_a + d
    o_qb = o_za + d
    o_kb, o_vb = o_qb + d, o_qb + 2 * d
    o_zb = o_qb + 3 * d
    o_ga, o_gb = o_zb + d, o_zb + 2 * d
    o_sm = o_gb + d

    it = iter(refs)
    x_ref, mod_ref, g1_ref, win_ref, wat_ref, wgk_ref, bgk_ref, wconv_ref = (next(it) for _ in range(8))
    alane_ref, dlane_ref, acol_ref, dcol_ref = (next(it) for _ in range(4))
    gna_ref, gnb_ref, wpa_ref, wpb_ref, wout_ref, gfin_ref, bd_ref, ut_ref = (next(it) for _ in range(8))
    if has_state:
        sgla_in, sgdn_in, conv_in = (next(it) for _ in range(3))
    y_ref, sgla_ref, sgdn_ref, conv_ref = (next(it) for _ in range(4))
    (hb_all, qe_ref, ke_ref, qd_ref, kd_ref, va_ref, sza_ref, bdec_ref, ubuf_ref,
     qn_ref, kn_ref, qdec_ref, kdec_ref, bv_ref, bek_ref, szb_ref,
     bcol_ref, betac_ref, brow_ref, dvec_ref, uv_ref, wk_ref, qkm_ref, ba_ref, oa_ref, ob_ref) = (next(it) for _ in range(26))

    hb_ref = hb_all.at[s]
    x_t = x_ref.at[0, pl.ds(s * tc, tc)]
    y_t = y_ref.at[0, pl.ds(s * tc, tc)]

    def _init():
        if has_state:
            sgla_ref[...] = sgla_in[...]
            sgdn_ref[...] = sgdn_in[...]
            ubuf_ref[HIST - (CONV_W - 1):HIST, :] = conv_in[0]
        else:
            sgla_ref[...] = jnp.zeros_like(sgla_ref)
            sgdn_ref[...] = jnp.zeros_like(sgdn_ref)
            ubuf_ref[0:HIST, :] = jnp.zeros((HIST, ubuf_ref.shape[1]), F32)

    if s == 0:
        pl.when(pl.program_id(1) == 0)(_init)

    masked = n_valid < tc
    if masked:
        rowmask = lax.broadcasted_iota(jnp.int32, (tc, 1), 0) < n_valid

    def mrow(v):
        return jnp.where(rowmask, v, 0.0) if masked else v

    x = x_t[...]
    mod = mod_ref[0]
    shift, scale = mod[:, 0:d], mod[:, d:2 * d]
    hn = x * lax.rsqrt(jnp.mean(x * x, axis=-1, keepdims=True) + EPS) * g1_ref[...]
    hb_ref[...] = (hn * (1.0 + scale) + shift).astype(BF16)

    def proj(c0, width):
        return _dot(hb_ref[...], win_ref[:, c0:c0 + width])

    pair_w = 2 * GDN_DK
    half = d // 2

    def gdn_proj(jp, parts=(0, 1, 2)):
        for j in parts:
            c0 = j * d + jp * pair_w
            ubuf_ref[HIST:HIST + tc, c0:c0 + pair_w] = proj(o_qb + c0, pair_w)

    def va_task(i):
        va_ref[:, i * half:(i + 1) * half] = mrow(proj(o_va + i * half, half)).astype(BF16)

    def silu_task(dst_ref, c0, i):
        dst_ref[:, i * half:(i + 1) * half] = _silu(proj(c0 + i * half, half))

    ps = proj(o_sm, LANES)
    arows = [_dot_nt(wat_ref[...], hb_ref[c * CHUNK:(c + 1) * CHUNK, :]) for c in range(nch)]
    gdn_proj(0, (0,))
    lane = lax.broadcasted_iota(jnp.int32, (1, LANES), 1)
    betac_ref[...] = mrow(jax.nn.sigmoid(ps))
    g_col = -jnp.exp(alane_ref[...]) * _softplus(ps + dlane_ref[...])
    g_col = mrow(jnp.where((lane >= AIN_LANE) & (lane < AIN_LANE + GDN_HEADS), g_col, 0.0))
    g_rows = []
    for c in range(nch):
        g_row = -jnp.exp(acol_ref[...]) * _softplus(arows[c] + dcol_ref[...])
        if masked:
            colmask = (lax.broadcasted_iota(jnp.int32, (1, CHUNK), 1) + c * CHUNK) < n_valid
            g_row = jnp.where(colmask, g_row, 0.0)
        g_rows.append(g_row)
    gdn_proj(0, (1,))
    bd = bd_ref[...]
    ut = ut_ref[...]
    gh, gm, gl = _split3(g_col)
    b_col = _dot(bd, gh) + _dot(bd, gm) + _dot(bd, gl)
    b_rows = []
    for c in range(nch):
        rh, rm, rl = _split3(g_rows[c])
        b_rows.append(_dot(rh, ut) + _dot(rm, ut) + _dot(rl, ut))
    gdn_proj(0, (2,))
    va_task(0)
    bcol_ref[...] = b_col
    for c in range(nch):
        brow_ref[c] = b_rows[c]
        dvec_ref[c] = jnp.broadcast_to(jnp.exp(b_rows[c][:, CHUNK - 1:CHUNK]), (GDN_HEADS, LANES))
    e_b = jnp.exp(b_col)
    b_col3 = b_col.reshape(nch, CHUNK, LANES)
    e_lb = jnp.exp(b_col3[:, CHUNK - 1:CHUNK, :] - b_col3).reshape(tc, LANES)
    beta = betac_ref[...]
    va_task(1)

    def conv(c0, width):
        cols = slice(c0, c0 + width)
        full = ubuf_ref[:, cols]
        acc = full[HIST:] * wconv_ref[CONV_W - 1:CONV_W, cols]
        for i in range(CONV_W - 1):
            back = CONV_W - 1 - i
            acc = acc + pltpu.roll(full, back, 0)[HIST:] * wconv_ref[i:i + 1, cols]
        return _silu(acc)

    def gdn_prep(h):
        cols = slice(h * GDN_DK, (h + 1) * GDN_DK)
        qh, kh_, vh = (conv(j * d + h * GDN_DK, GDN_DK) for j in range(3))
        be_h = beta[:, BETA_LANE + h:BETA_LANE + h + 1]
        eb_h = e_b[:, AIN_LANE + h:AIN_LANE + h + 1]
        elb_h = e_lb[:, AIN_LANE + h:AIN_LANE + h + 1]
        qh = mrow(qh * lax.rsqrt(jnp.sum(qh * qh, axis=-1, keepdims=True) + EPS) * (GDN_DK ** -0.5))
        kh_ = mrow(kh_ * lax.rsqrt(jnp.sum(kh_ * kh_, axis=-1, keepdims=True) + EPS))
        vh = mrow(vh)
        qn_ref[:, cols] = qh.astype(BF16)
        kn_ref[:, cols] = kh_.astype(BF16)
        qdec_ref[:, cols] = (qh * eb_h).astype(BF16)
        kdec_ref[:, cols] = (kh_ * elb_h).astype(BF16)
        bv_ref[:, cols] = (be_h * vh).astype(BF16)
        bek_ref[:, cols] = ((be_h * eb_h) * kh_).astype(BF16)

    def gla_gate_task():
        gk = _log_sigmoid(_dot(ps.astype(BF16), wgk_ref[...]) + bgk_ref[...]) * (1.0 / GLA_GATE_NORM)
        gk = mrow(gk)
        kh, kl = _split2(gk)
        ba_ref[...] = _dot(bd, kh) + _dot(bd, kl)

    def gla_qk_task(which):
        b_a = ba_ref[...].reshape(nch, CHUNK, qk_a)
        b_mid = b_a[:, CHUNK // 2 - 1:CHUNK // 2, :]
        b_last = b_a[:, CHUNK - 1:CHUNK, :]
        if which == 0:
            bdec_ref[...] = jnp.exp(b_last)
            qa = mrow(proj(o_qa, qk_a) * (dk_a ** -0.5)).reshape(nch, CHUNK, qk_a)
            qe_ref[...] = (qa * jnp.exp(b_a - b_mid)).reshape(tc, qk_a).astype(BF16)
            qd_ref[...] = (qa * jnp.exp(b_a)).reshape(tc, qk_a).astype(BF16)
        else:
            ka = mrow(proj(o_ka, qk_a)).reshape(nch, CHUNK, qk_a)
            ke_ref[...] = (ka * jnp.exp(b_mid - b_a)).reshape(tc, qk_a).astype(BF16)
            kd_ref[...] = (ka * jnp.exp(b_last - b_a)).reshape(tc, qk_a).astype(BF16)

    side_tasks = [gla_gate_task,
                  functools.partial(gla_qk_task, 0), functools.partial(gla_qk_task, 1),
                  functools.partial(silu_task, sza_ref, o_za, 0), functools.partial(silu_task, sza_ref, o_za, 1),
                  functools.partial(silu_task, szb_ref, o_zb, 0), functools.partial(silu_task, szb_ref, o_zb, 1)]

    def side(n=1):
        for _ in range(n):
            if side_tasks:
                side_tasks.pop(0)()

    n_pairs = GDN_HEADS // 2
    for jp in range(n_pairs):
        more = jp + 1 < n_pairs
        if more:
            gdn_proj(jp + 1, (0,))
        gdn_prep(2 * jp)
        if more:
            gdn_proj(jp + 1, (1,))
        side()
        gdn_prep(2 * jp + 1)
        if more:
            gdn_proj(jp + 1, (2,))
        side()

    tail = ubuf_ref[HIST + n_valid - (CONV_W - 1):HIST + n_valid, :]
    ubuf_ref[HIST - (CONV_W - 1):HIST, :] = tail
    conv_ref[0] = tail

    npair = GDN_HEADS // 2
    ri = lax.broadcasted_iota(jnp.int32, (CHUNK, 2 * CHUNK), 0)
    li = lax.broadcasted_iota(jnp.int32, (CHUNK, 2 * CHUNK), 1)
    ci = li & (CHUNK - 1)
    lo = li < CHUNK
    incl = ri >= ci
    strict = ri > ci
    eye = jnp.where(ri == ci, 1.0, 0.0).astype(F32)
    blk = {}
    s_ = INV_BASE
    while s_ <= CHUNK:
        sh = s_.bit_length() - 1
        blk[s_] = (ri >> sh) == (ci >> sh)
        s_ *= 2
    incl1 = (lax.broadcasted_iota(jnp.int32, (CHUNK, CHUNK), 0)
             >= lax.broadcasted_iota(jnp.int32, (CHUNK, CHUNK), 1))
    gna = gna_ref[...]
    gnb = gnb_ref[...]
    zblk = jnp.zeros((CHUNK, LANES), BF16)

    def bdiag_packed(y):
        return jnp.concatenate([jnp.where(lo, y, 0), jnp.where(lo, 0, y)], axis=0)

    def bdiag_wide(y):
        return jnp.concatenate([jnp.concatenate([y[:, 0:LANES], zblk], axis=1),
                                jnp.concatenate([zblk, y[:, LANES:2 * LANES]], axis=1)], axis=0)

    def chunk_rows(c):
        return pl.ds(pl.multiple_of(c * CHUNK, CHUNK), CHUNK)

    chains = [(c, p) for c in range(nch) for p in range(npair)]
    crow = [slice(c * CHUNK, (c + 1) * CHUNK) for c in range(nch)]
    pcols = lambda p: slice(p * pair_w, (p + 1) * pair_w)
    kbds = [bdiag_wide(kn_ref[crow[c], pcols(p)]) for c, p in chains]
    kks = [_dot_nt(kn_ref[crow[c], pcols(p)], kbd) for (c, p), kbd in zip(chains, kbds)]
    qks = [_dot_nt(qn_ref[crow[c], pcols(p)], kbd) for (c, p), kbd in zip(chains, kbds)]
    bcs = [bcol_ref[crow[c], :] for c in range(nch)]
    bes = [betac_ref[crow[c], :] for c in range(nch)]
    brs = [brow_ref[c] for c in range(nch)]
    side()
    a_s, tinvs, pws = [], [], []
    for n_, (c, p) in enumerate(chains):
        h1, h2 = 2 * p, 2 * p + 1
        pick = lambda v, l0: jnp.where(lo, v[:, l0 + h1:l0 + h1 + 1], v[:, l0 + h2:l0 + h2 + 1])
        diff = pick(bcs[c], AIN_LANE) - jnp.where(lo[0:1], brs[c][h1:h1 + 1, :], brs[c][h2:h2 + 1, :])
        dm = jnp.where(incl, jnp.exp(jnp.where(incl, diff, 0.0)), 0.0)
        qkm_ref[c, p] = (qks[n_] * dm).astype(BF16)
        a = jnp.where(strict, pick(bes[c], BETA_LANE) * kks[n_] * dm, 0.0)
        dblk = jnp.where(blk[INV_BASE], a, 0.0)
        a_s.append(a)
        tinvs.append(eye - dblk)
        pws.append(dblk.astype(BF16))
    side()
    n = 2
    while n < INV_BASE:
        pws = [_dot(p_, bdiag_packed(p_)).astype(BF16) for p_ in pws]
        tinvs = [t_ + _dot(t_.astype(BF16), bdiag_packed(p_)) for t_, p_ in zip(tinvs, pws)]
        side()
        n *= 2
    while n < CHUNK:
        es = [jnp.where(blk[2 * n] & ~blk[n], a, 0.0).astype(BF16) for a in a_s]
        tbs = [t_.astype(BF16) for t_ in tinvs]
        tbds = [bdiag_packed(tb) for tb in tbs]
        tes = [_dot(tb, bdiag_packed(e)).astype(BF16) for tb, e in zip(tbs, es)]
        side()
        tinvs = [t_ - _dot(te, tbd) for t_, te, tbd in zip(tinvs, tes, tbds)]
        side()
        n *= 2
    for t_, (c, p) in zip(tinvs, chains):
        tb = t_.astype(BF16)
        uv_ref[crow[c], pcols(p)] = _dot(tb, bdiag_wide(bv_ref[crow[c], pcols(p)]))
        wk_ref[crow[c], pcols(p)] = _dot(tb, bdiag_wide(bek_ref[crow[c], pcols(p)])).astype(BF16)
    side(len(side_tasks))

    def chunk_body(c, carry):
        rows = chunk_rows(c)
        ha, hb_ = range(GLA_HEADS), range(GDN_HEADS)
        lk = lambda h: slice(h * dk_a, (h + 1) * dk_a)
        lv = lambda h: slice(h * dv_a, (h + 1) * dv_a)
        hc = lambda h: slice(h * GDN_DK, (h + 1) * GDN_DK)
        sa = [sgla_ref[0, h] for h in ha]
        sb = [sgdn_ref[0, h] for h in hb_]
        v_a = [va_ref[rows, lv(h)] for h in ha]
        dec_a = bdec_ref[c]
        dv_all = dvec_ref[c]
        sab = [s_.astype(BF16) for s_ in sa]
        sbb = [s_.astype(BF16) for s_ in sb]
        wq = [_dot(jnp.concatenate([wk_ref[rows, hc(h)], qdec_ref[rows, hc(h)]], axis=0), sbb[h]) for h in hb_]
        ws = [w_[0:CHUNK] for w_ in wq]
        qsb = [w_[CHUNK:2 * CHUNK] for w_ in wq]
        att = [jnp.where(incl1, _dot_nt(qe_ref[rows, lk(h)], ke_ref[rows, lk(h)]), 0.0).astype(BF16) for h in ha]
        oi = [_dot(qd_ref[rows, lk(h)], sab[h]) for h in ha]
        kv = [_dot_tn(kd_ref[rows, lk(h)], v_a[h]) for h in ha]
        u = [(uv_ref[rows, hc(h)] - ws[h]).astype(BF16) for h in hb_]
        qku = [_dot(qkm_ref[c, p], bdiag_wide(jnp.concatenate([u[2 * p], u[2 * p + 1]], axis=1)))
               for p in range(npair)]
        o_b = [qsb[h] + qku[h // 2][:, (h % 2) * GDN_DV:(h % 2 + 1) * GDN_DV] for h in hb_]
        sb_new = [dv_all[h:h + 1, :] * sb[h] + _dot_tn(kdec_ref[rows, hc(h)], u[h]) for h in hb_]
        o_a = [_dot(att[h], v_a[h]) + oi[h] for h in ha]
        sa_new = []
        for h in ha:
            dcol = jnp.broadcast_to(dec_a[:, lk(h)], (dk_a, dk_a)).T
            sa_new.append(jnp.concatenate([dcol] * (dv_a // dk_a), axis=1) * sa[h] + kv[h])
        for h in ha:
            o = o_a[h]
            o = o * lax.rsqrt(jnp.mean(o * o, axis=-1, keepdims=True) + EPS) * gna
            oa_ref[rows, lv(h)] = (o * sza_ref[rows, lv(h)]).astype(BF16)
            sgla_ref[0, h] = sa_new[h]
        for h in hb_:
            o = o_b[h]
            o = o * lax.rsqrt(jnp.mean(o * o, axis=-1, keepdims=True) + EPS) * gnb
            ob_ref[rows, hc(h)] = (o * szb_ref[rows, hc(h)]).astype(BF16)
            sgdn_ref[0, h] = sb_new[h]
        return carry

    lax.fori_loop(0, nch, chunk_body, 0, unroll=True)

    merged = (jax.nn.sigmoid(proj(o_ga, d)) * _dot(oa_ref[...], wpa_ref[...])
              + jax.nn.sigmoid(proj(o_gb, d)) * _dot(ob_ref[...], wpb_ref[...]))
    out = _dot(merged.astype(BF16), wout_ref[...])
    xn = x_t[...] + mod_ref[0][:, 2 * d:3 * d] * out
    if final_norm:
        xn = xn * lax.rsqrt(jnp.mean(xn * xn, axis=-1, keepdims=True) + EPS) * gfin_ref[...]
    y_t[...] = xn


def _resident(shape):
    zeros = (0,) * len(shape)
    return pl.BlockSpec(shape, lambda b, t: zeros, pipeline_mode=pl.Buffered(1))


def _layer(x, mod, consts, states, *, tc, sub, n_valid, final_norm):
    bsz, t_len, d = x.shape
    nch = tc // CHUNK
    qk_a = d // 2
    has_state = states is not None
    const_specs = [_resident(c.shape) for c in consts]
    in_specs = [pl.BlockSpec((1, sub * tc, d), lambda b, t: (b, t, 0)),
                pl.BlockSpec((1, 1, 3 * d), lambda b, t: (b, 0, 0))] + const_specs
    args = [x, mod.reshape(bsz, 1, 3 * d)] + list(consts)
    state_shapes = [(bsz, GLA_HEADS, qk_a // GLA_HEADS, d // GLA_HEADS),
                    (bsz, GDN_HEADS, GDN_DK, GDN_DV),
                    (bsz, CONV_W - 1, 3 * d)]
    state_specs = [pl.BlockSpec((1,) + s[1:], lambda b, t, n=len(s): (b,) + (0,) * (n - 1)) for s in state_shapes]
    if has_state:
        in_specs += state_specs
        args += list(states)
    scratch = [
        pltpu.VMEM((sub, tc, d), BF16),
        pltpu.VMEM((tc, qk_a), BF16), pltpu.VMEM((tc, qk_a), BF16),
        pltpu.VMEM((tc, qk_a), BF16), pltpu.VMEM((tc, qk_a), BF16),
        pltpu.VMEM((tc, d), BF16),
        pltpu.VMEM((tc, d), F32),
        pltpu.VMEM((nch, 1, qk_a), F32),
        pltpu.VMEM((HIST + tc, 3 * d), F32),
        pltpu.VMEM((tc, d), BF16), pltpu.VMEM((tc, d), BF16),
        pltpu.VMEM((tc, d), BF16), pltpu.VMEM((tc, d), BF16),
        pltpu.VMEM((tc, d), BF16), pltpu.VMEM((tc, d), BF16),
        pltpu.VMEM((tc, d), F32),
        pltpu.VMEM((tc, LANES), F32), pltpu.VMEM((tc, LANES), F32),
        pltpu.VMEM((nch, GDN_HEADS, 2 * CHUNK), F32),
        pltpu.VMEM((nch, GDN_HEADS, LANES), F32),
        pltpu.VMEM((tc, d), F32), pltpu.VMEM((tc, d), BF16),
        pltpu.VMEM((nch, GDN_HEADS // 2, CHUNK, 2 * CHUNK), BF16),
        pltpu.VMEM((tc, qk_a), F32),
        pltpu.VMEM((tc, d), BF16), pltpu.VMEM((tc, d), BF16),
    ]
    kern = functools.partial(_layer_kernel, sub=sub, tc=tc, n_valid=n_valid, has_state=has_state,
                             final_norm=final_norm, d_model=d)
    return pl.pallas_call(
        kern,
        grid=(bsz, t_len // (sub * tc)),
        in_specs=in_specs,
        out_specs=[pl.BlockSpec((1, sub * tc, d), lambda b, t: (b, t, 0))] + state_specs,
        out_shape=[jax.ShapeDtypeStruct(x.shape, F32)] + [jax.ShapeDtypeStruct(s, F32) for s in state_shapes],
        scratch_shapes=scratch,
        compiler_params=pltpu.CompilerParams(dimension_semantics=("arbitrary", "arbitrary"),
                                             vmem_limit_bytes=VMEM_LIMIT_BYTES),
        name="gla_gdn_layer",
    )(*args)


def _permute_cast_kernel(w_ref, o_ref, *, segments, pad):
    w = w_ref[...]
    parts = [w[:, a:b] for a, b in segments] + [jnp.zeros((w.shape[0], pad), w.dtype)]
    o_ref[...] = jnp.concatenate(parts, axis=1).astype(o_ref.dtype)


def _permute_cast(w, segments, pad):
    rows, cols = w.shape
    out_cols = sum(b - a for a, b in segments) + pad
    bm = PREP_ROWS
    return pl.pallas_call(
        functools.partial(_permute_cast_kernel, segments=segments, pad=pad),
        grid=(rows // bm,),
        in_specs=[pl.BlockSpec((bm, cols), lambda i: (i, 0))],
        out_specs=pl.BlockSpec((bm, out_cols), lambda i: (i, 0)),
        out_shape=jax.ShapeDtypeStruct((rows, out_cols), BF16),
        name="permute_cast",
    )(w)


def _layer_consts(g_norm1, w_in, w_gk2, b_gk, w_conv, a_log, dt_bias, g_norm_a, g_norm_b,
                  w_pa, w_pb, w_out, g_final):
    d = w_in.shape[0]
    qk_a = d // 2
    o_gk = 2 * qk_a + 2 * d
    o_qkv = o_gk + GLA_RANK
    o_zb = o_qkv + 3 * d
    o_beta = o_zb + d
    o_a = o_beta + GDN_HEADS
    o_ga = o_a + GDN_HEADS
    pad = LANES - (GLA_RANK + 2 * GDN_HEADS)
    w_perm = _permute_cast(w_in, ((0, o_gk), (o_qkv, o_beta), (o_ga, o_ga + 2 * d), (o_gk, o_qkv), (o_beta, o_ga)), pad)
    w_at = w_in[:, o_a:o_ga].T.astype(BF16)
    w_gk = jnp.zeros((LANES, qk_a), F32).at[0:GLA_RANK].set(w_gk2).astype(BF16)
    lane_vec = lambda v: jnp.zeros((1, LANES), F32).at[0, AIN_LANE:AIN_LANE + GDN_HEADS].set(v)
    col_vec = lambda v: jnp.broadcast_to(v.reshape(GDN_HEADS, 1), (GDN_HEADS, CHUNK)).astype(F32)
    tu = jnp.arange(CHUNK)
    ut = jnp.tile(tu[:, None] <= tu[None, :], (1, 2)).astype(BF16)
    return [g_norm1.reshape(1, d), w_perm, w_at, w_gk, b_gk.reshape(1, qk_a), w_conv,
            lane_vec(a_log), lane_vec(dt_bias), col_vec(a_log), col_vec(dt_bias),
            g_norm_a.reshape(1, -1), g_norm_b.reshape(1, -1),
            w_pa.astype(BF16), w_pb.astype(BF16), w_out.astype(BF16), g_final.reshape(1, d), ut]


def _tile_consts(consts, tc):
    tt = jnp.arange(tc)
    bd = ((tt[:, None] // CHUNK == tt[None, :] // CHUNK) & (tt[None, :] <= tt[:, None])).astype(BF16)
    return consts[:-1] + [bd, consts[-1]]


PROMPT_TILE = 256
PROMPT_SUB = 2


def kernel(x_prompt, x_sample, c_prompt, c_sample, state_gla, state_gdn, cache_conv_gdn, w_ada, b_ada, g_norm1, w_in, w_gk2, b_gk, w_conv, a_log, dt_bias, g_norm_a, g_norm_b, w_pa, w_pb, w_out, g_final):
    depth = w_in.shape[0]
    bp, tp, _ = x_prompt.shape
    bs, ts, _ = x_sample.shape
    assert tp % (PROMPT_SUB * PROMPT_TILE) == 0 and CONV_W - 1 <= ts <= CHUNK
    hp = x_prompt
    hs = jnp.pad(x_sample, ((0, 0), (0, CHUNK - ts), (0, 0)))
    outs_p, outs_s = [], []
    for layer in range(depth):
        last = layer == depth - 1
        mod = _adaln_mod(jnp.concatenate([c_prompt, c_sample], axis=0), w_ada[layer], b_ada[layer])
        lw = (g_norm1[layer], w_in[layer], w_gk2[layer], b_gk[layer], w_conv[layer], a_log[layer],
              dt_bias[layer], g_norm_a[layer], g_norm_b[layer], w_pa[layer], w_pb[layer], w_out[layer], g_final)
        consts = _layer_consts(*lw)
        hp, *st_p = _layer(hp, mod[:bp], _tile_consts(consts, PROMPT_TILE), None,
                           tc=PROMPT_TILE, sub=PROMPT_SUB, n_valid=PROMPT_TILE, final_norm=last)
        st_in = (state_gla[layer], state_gdn[layer], cache_conv_gdn[layer])
        hs, *st_s = _layer(hs, mod[bp:], _tile_consts(consts, CHUNK), st_in,
                           tc=CHUNK, sub=1, n_valid=ts, final_norm=last)
        outs_p.append(st_p)
        outs_s.append(st_s)

    stack = lambda outs, i: jnp.stack([o[i] for o in outs])
    return (hp, hs[:, :ts], stack(outs_p, 0), stack(outs_p, 1), stack(outs_p, 2),
            stack(outs_s, 0), stack(outs_s, 1), stack(outs_s, 2))
```
